```python
import jax, jax.numpy as jnp
from jax import lax
import numpy as np

D_MODEL = 1024
BATCH = 16
SEQ = 4096
DEPTH = 4
DEC_BATCH = 4
DEC_SEQ = 4096
PAST_LEN = 128

GRID_W = 64
GLA_HEADS = 4
GLA_DK = 64
GLA_DV = 128
GLA_KEY = GLA_HEADS * GLA_DK
GLA_VAL = GLA_HEADS * GLA_DV
GLA_RANK = 16
GLA_TAU = 16.0
GLA_CHUNK = 64
POOL_GROUPS = 4
POOL_GC = 64
POOL_WIDTH = POOL_GROUPS * POOL_GC
POOL_WINDOWS = (2, 4, 8, 16)
NA_HEADS = 4
NA_HD = 64
NA_WIDTH = NA_HEADS * NA_HD
NA_WIN_R = 8
NA_WIN_C = 16
N_BRANCH = 3
D_FF = -(-8 * D_MODEL // (3 * 256)) * 256
DEEPNORM_ALPHA = (2 * DEPTH) ** 0.25
DEEPNORM_BETA = (8 * DEPTH) ** -0.25
LN_EPS = 1e-5
RMS_EPS = 1e-6

IN_SIZES = (GLA_KEY, GLA_KEY, GLA_VAL, GLA_VAL, 2 * GLA_RANK, POOL_WIDTH, 3 * NA_WIDTH, N_BRANCH * D_MODEL)
D_IN = sum(IN_SIZES)
IN_OFFSETS = [int(v) for v in np.cumsum(IN_SIZES)[:-1]]

kernel_name = 'hybrid_gla_pool_natten_encoder'


def layer_norm(x, g, b):
    xf = x.astype(jnp.float32)
    mu = jnp.mean(xf, axis=-1, keepdims=True)
    var = jnp.mean(jnp.square(xf - mu), axis=-1, keepdims=True)
    return ((xf - mu) * lax.rsqrt(var + LN_EPS) * g + b).astype(x.dtype)


def gla_one_direction(q, k, v, log_a):
    B, T, H, dk = q.shape
    dv = v.shape[-1]
    L = GLA_CHUNK
    n = T // L
    q, k, log_a = (a.reshape(B, n, L, H, dk) for a in (q, k, log_a))
    v = v.reshape(B, n, L, H, dv)
    b = jnp.cumsum(log_a, axis=2)
    b_end = b[:, :, -1]
    q_dec = q * jnp.exp(b)
    k_inv = k * jnp.exp(-b)
    att = jnp.einsum('bnlhd,bnmhd->bnhlm', q_dec, k_inv)
    att = jnp.where(jnp.tril(jnp.ones((L, L), dtype=bool)), att, 0.0)
    o_intra = jnp.einsum('bnhlm,bnmhe->bnlhe', att, v)
    k_end = k * jnp.exp(b_end[:, :, None] - b)
    ds = jnp.einsum('bnlhd,bnlhe->nbhde', k_end, v)
    decay = jnp.transpose(jnp.exp(b_end), (1, 0, 2, 3))

    def step(s, inp):
        dcy, d_s = inp
        return dcy[..., None] * s + d_s, s

    _, s_prev = lax.scan(step, jnp.zeros((B, H, dk, dv), q.dtype), (decay, ds))
    o_inter = jnp.einsum('bnlhd,nbhde->bnlhe', q_dec, s_prev)
    return (o_intra + o_inter).reshape(B, T, H, dv)


def gla_mixer(q, k, v, ogate, lr, up_f, up_b, bias_f, bias_b, norm_g):
    B, T, _ = q.shape
    f32 = jnp.float32
    lr = lr.astype(f32)
    lr_f, lr_b = lr[..., :GLA_RANK], lr[..., GLA_RANK:]
    heads_k = lambda a: a.reshape(B, T, GLA_HEADS, GLA_DK)
    log_a_f = heads_k(jax.nn.log_sigmoid(lr_f @ up_f + bias_f) / GLA_TAU)
    log_a_b = heads_k(jax.nn.log_sigmoid(lr_b @ up_b + bias_b) / GLA_TAU)
    qh = heads_k(q.astype(f32)) * (GLA_DK ** -0.5)
    kh = heads_k(k.astype(f32))
    vh = v.astype(f32).reshape(B, T, GLA_HEADS, GLA_DV)
    o_f = gla_one_direction(qh, kh, vh, log_a_f)
    flip = lambda a: jnp.flip(a, axis=1)
    o_b = flip(gla_one_direction(flip(qh), flip(kh), flip(vh), flip(log_a_b)))
    o = o_f + o_b
    o = o * lax.rsqrt(jnp.mean(jnp.square(o), axis=-1, keepdims=True) + RMS_EPS) * norm_g
    return o.reshape(B, T, GLA_VAL) * jax.nn.silu(ogate.astype(f32))


def pool_mixer(p, pool_w, pool_scale):
    B, T, C = p.shape
    pf = p.astype(jnp.float32)
    csum = jnp.concatenate([jnp.zeros((B, 1, C), jnp.float32), jnp.cumsum(pf, axis=1)], axis=1)
    t = jnp.arange(T)
    outs = []
    for gi, w in enumerate(POOL_WINDOWS):
        sl = slice(gi * POOL_GC, (gi + 1) * POOL_GC)
        lo = jnp.clip(t - w // 2, 0, T)
        hi = jnp.clip(t + w - w // 2, 0, T)
        cnt = (hi - lo).astype(jnp.float32)[:, None]
        mean = (csum[:, hi, sl] - csum[:, lo, sl]) / cnt
        outs.append((mean - pf[..., sl]) @ pool_w[gi])
    return jnp.concatenate(outs, axis=-1) * pool_scale


def neighborhood_attention(q, k, v, rpb):
    B, T, _ = q.shape
    rows = T // GRID_W
    wr = min(NA_WIN_R, rows)
    f32 = jnp.float32
    grid = lambda a: a.astype(f32).reshape(B, rows, GRID_W, NA_HEADS, NA_HD)
    qg = grid(q) * (NA_HD ** -0.5)
    kg, vg = grid(k), grid(v)
    cols = np.arange(GRID_W)
    c0 = np.clip(cols - NA_WIN_C // 2, 0, GRID_W - NA_WIN_C)
    col_idx = c0[:, None] + np.arange(NA_WIN_C)[None, :]
    col_off = col_idx - cols[:, None] + (NA_WIN_C - 1)
    rpb_c = rpb[:, :, col_off]

    def row_block(r):
        r0 = jnp.clip(r - wr // 2, 0, rows - wr)
        kb = lax.dynamic_slice_in_dim(kg, r0, wr, axis=1)
        vb = lax.dynamic_slice_in_dim(vg, r0, wr, axis=1)
        ks = kb[:, :, col_idx]
        vs = vb[:, :, col_idx]
        qr = lax.dynamic_index_in_dim(qg, r, axis=1, keepdims=False)
        s = jnp.einsum('bqhd,brqjhd->bhqrj', qr, ks)
        row_off = r0 + jnp.arange(wr) - r + (NA_WIN_R - 1)
        bias = jnp.take(rpb_c, row_off, axis=1)
        s = s + jnp.transpose(bias, (0, 2, 1, 3))[None]
        pr = jax.nn.softmax(s.reshape(B, NA_HEADS, GRID_W, wr * NA_WIN_C), axis=-1).reshape(s.shape)
        return jnp.einsum('bhqrj,brqjhd->bqhd', pr, vs)

    out = lax.map(row_block, jnp.arange(rows))
    return jnp.transpose(out, (1, 0, 2, 3, 4)).reshape(B, T, NA_WIDTH)


def encoder_layer(x, w_in, gla_up_f, gla_up_b, gla_bias_f, gla_bias_b, gla_norm, pool_w, pool_scale,
                  na_rpb, w_br_a, w_br_b, w_br_c, w_out, ln1_g, ln1_b, w_gate, w_up, w_down, ln2_g, ln2_b):
    h = x @ w_in
    aq, ak, av, ag, alr, pin, cqkv, gates = jnp.split(h, IN_OFFSETS, axis=-1)
    ya = gla_mixer(aq, ak, av, ag, alr, gla_up_f, gla_up_b, gla_bias_f, gla_bias_b, gla_norm) @ w_br_a
    yb = pool_mixer(pin, pool_w, pool_scale) @ w_br_b
    cq, ck, cv = jnp.split(cqkv, 3, axis=-1)
    yc = neighborhood_attention(cq, ck, cv, na_rpb) @ w_br_c
    g = jax.nn.sigmoid(gates.astype(jnp.float32))
    m = (g[..., :D_MODEL] * ya + g[..., D_MODEL:2 * D_MODEL] * yb + g[..., 2 * D_MODEL:] * yc)
    x = layer_norm(DEEPNORM_ALPHA * x + (m @ w_out).astype(x.dtype), ln1_g, ln1_b)
    f = (jax.nn.silu(x @ w_gate) * (x @ w_up)) @ w_down
    x = layer_norm(DEEPNORM_ALPHA * x + f.astype(x.dtype), ln2_g, ln2_b)
    return x


def trunk(x, w_in, gla_up_f, gla_up_b, gla_bias_f, gla_bias_b, gla_norm, pool_w, pool_scale, na_rpb,
          w_br_a, w_br_b, w_br_c, w_out, ln1_g, ln1_b, w_gate, w_up, w_down, ln2_g, ln2_b):
    for l in range(DEPTH):
        x = encoder_layer(x, w_in[l], gla_up_f[l], gla_up_b[l], gla_bias_f[l], gla_bias_b[l], gla_norm[l],
                          pool_w[l], pool_scale[l], na_rpb[l], w_br_a[l], w_br_b[l], w_br_c[l], w_out[l],
                          ln1_g[l], ln1_b[l], w_gate[l], w_up[l], w_down[l], ln2_g[l], ln2_b[l])
    return x


def setup_inputs(seed: int = 0) -> dict:
    key = jax.random.key(seed)
    ks = jax.random.split(key, 24)
    f32 = jnp.float32
    nrm = lambda kk, shape, scale: jax.random.normal(kk, shape, f32) * scale
    D = D_MODEL
    return {
        'x_prompt': nrm(ks[0], (BATCH, SEQ, D), 1.0),
        'x_sample': nrm(ks[1], (DEC_BATCH, DEC_SEQ, D), 1.0),
        'w_in': nrm(ks[2], (DEPTH, D, D_IN), D ** -0.5),
        'gla_up_f': nrm(ks[3], (DEPTH, GLA_RANK, GLA_KEY), GLA_RANK ** -0.5),
        'gla_up_b': nrm(ks[4], (DEPTH, GLA_RANK, GLA_KEY), GLA_RANK ** -0.5),
        'gla_bias_f': nrm(ks[5], (DEPTH, GLA_KEY), 0.5),
        'gla_bias_b': nrm(ks[6], (DEPTH, GLA_KEY), 0.5),
        'gla_norm': 1.0 + nrm(ks[7], (DEPTH, GLA_DV), 0.02),
        'pool_w': nrm(ks[8], (DEPTH, POOL_GROUPS, POOL_GC, POOL_GC), POOL_GC ** -0.5),
        'pool_scale': 1.0 + nrm(ks[9], (DEPTH, POOL_WIDTH), 0.02),
        'na_rpb': nrm(ks[10], (DEPTH, NA_HEADS, 2 * NA_WIN_R - 1, 2 * NA_WIN_C - 1), 0.1),
        'w_br_a': nrm(ks[11], (DEPTH, GLA_VAL, D), GLA_VAL ** -0.5 * DEEPNORM_BETA),
        'w_br_b': nrm(ks[12], (DEPTH, POOL_WIDTH, D), POOL_WIDTH ** -0.5 * DEEPNORM_BETA),
        'w_br_c': nrm(ks[13], (DEPTH, NA_WIDTH, D), NA_WIDTH ** -0.5 * DEEPNORM_BETA),
        'w_out': nrm(ks[14], (DEPTH, D, D), D ** -0.5 * DEEPNORM_BETA),
        'ln1_g': 1.0 + nrm(ks[15], (DEPTH, D), 0.02),
        'ln1_b': nrm(ks[16], (DEPTH, D), 0.02),
        'w_gate': nrm(ks[17], (DEPTH, D, D_FF), D ** -0.5),
        'w_up': nrm(ks[18], (DEPTH, D, D_FF), D ** -0.5 * DEEPNORM_BETA),
        'w_down': nrm(ks[19], (DEPTH, D_FF, D), D_FF ** -0.5 * DEEPNORM_BETA),
        'ln2_g': 1.0 + nrm(ks[20], (DEPTH, D), 0.02),
        'ln2_b': nrm(ks[21], (DEPTH, D), 0.02),
    }


def reference(x_prompt, x_sample, w_in, gla_up_f, gla_up_b, gla_bias_f, gla_bias_b, gla_norm, pool_w,
              pool_scale, na_rpb, w_br_a, w_br_b, w_br_c, w_out, ln1_g, ln1_b, w_gate, w_up, w_down,
              ln2_g, ln2_b):
    y_prompt = trunk(x_prompt, w_in, gla_up_f, gla_up_b, gla_bias_f, gla_bias_b, gla_norm, pool_w,
                     pool_scale, na_rpb, w_br_a, w_br_b, w_br_c, w_out, ln1_g, ln1_b, w_gate, w_up,
                     w_down, ln2_g, ln2_b)
    y_sample = trunk(x_sample, w_in, gla_up_f, gla_up_b, gla_bias_f, gla_bias_b, gla_norm, pool_w,
                     pool_scale, na_rpb, w_br_a, w_br_b, w_br_c, w_out, ln1_g, ln1_b, w_gate, w_up,
                     w_down, ln2_g, ln2_b)
    return (y_prompt, y_sample)
```

```python
import functools

import numpy as np
import jax
import jax.numpy as jnp
from jax import lax
from jax.experimental import pallas as pl
from jax.experimental.pallas import tpu as pltpu

f32 = jnp.float32
bf16 = jnp.bfloat16

D_MODEL = 1024
DEPTH = 4
GRID_W = 64
GLA_HEADS = 4
GLA_DK = 64
GLA_DV = 128
GLA_KEY = GLA_HEADS * GLA_DK
GLA_VAL = GLA_HEADS * GLA_DV
GLA_RANK = 16
GLA_TAU = 16.0
GLA_CHUNK = 64
POOL_GROUPS = 4
POOL_GC = 64
POOL_WIDTH = POOL_GROUPS * POOL_GC
POOL_WINDOWS = (2, 4, 8, 16)
NA_HEADS = 4
NA_HD = 64
NA_WIDTH = NA_HEADS * NA_HD
NA_WIN_R = 8
NA_WIN_C = 16
N_BRANCH = 3
D_FF = 2816
DEEPNORM_ALPHA = (2 * DEPTH) ** 0.25
LN_EPS = 1e-5
RMS_EPS = 1e-6
LR_PAD = 128
NEG_BIG = -1e30

VMEM_LIMIT_BYTES = 56 * 1024 * 1024

_NN = (((1,), (0,)), ((), ()))
_NT = (((1,), (1,)), ((), ()))
_TN = (((0,), (0,)), ((), ()))


def _dot(a, b, dims=_NN):
    return lax.dot_general(a, b, dims, preferred_element_type=f32)


def _const_spec(shape):
    nd = len(shape)
    return pl.BlockSpec(shape, lambda *_: (0,) * nd, pipeline_mode=pl.Buffered(1))


def _params(sem):
    return pltpu.CompilerParams(dimension_semantics=sem, vmem_limit_bytes=VMEM_LIMIT_BYTES)


def _layer_norm(y, g, b):
    mu = jnp.mean(y, axis=-1, keepdims=True)
    yc = y - mu
    var = jnp.mean(yc * yc, axis=-1, keepdims=True)
    return yc * lax.rsqrt(var + LN_EPS) * g + b


def _sigmoid(x):
    return 1.0 / (1.0 + jnp.exp(-x))


def _inproj_kernel(x_ref, wa_ref, wlr_ref, wb_ref, wg_ref,
                   qk_ref, v_ref, og_ref, lr_ref, p_ref, qkv_ref, g_ref):
    xb = x_ref[...].astype(bf16)
    ra = _dot(xb, wa_ref[...])
    qk_ref[...] = ra[:, :2 * GLA_KEY].astype(bf16)
    v_ref[...] = ra[:, 2 * GLA_KEY:2 * GLA_KEY + GLA_VAL].astype(bf16)
    og_ref[...] = ra[:, 2 * GLA_KEY + GLA_VAL:].astype(bf16)
    lr_ref[...] = _dot(xb, wlr_ref[...]).astype(bf16)
    rb = _dot(xb, wb_ref[...])
    p_ref[...] = rb[:, :POOL_WIDTH].astype(bf16)
    qkv_ref[...] = rb[:, POOL_WIDTH:].astype(bf16)
    g_ref[...] = _dot(xb, wg_ref[...]).astype(bf16)


def _inproj(x, wa, wlr, wb, wg, tm):
    B, T, D = x.shape
    widths = (2 * GLA_KEY, GLA_VAL, GLA_VAL, LR_PAD, POOL_WIDTH, 3 * NA_WIDTH, N_BRANCH * D_MODEL)
    tok = lambda w: pl.BlockSpec((None, tm, w), lambda b, i: (b, i, 0))
    return pl.pallas_call(
        _inproj_kernel,
        grid=(B, T // tm),
        in_specs=[tok(D), _const_spec(wa.shape), _const_spec(wlr.shape), _const_spec(wb.shape),
                  _const_spec(wg.shape)],
        out_specs=[tok(w) for w in widths],
        out_shape=[jax.ShapeDtypeStruct((B, T, w), bf16) for w in widths],
        compiler_params=_params(("parallel", "parallel")),
        name="inproj",
    )(x, wa, wlr, wb, wg)


def _gla_chunk(c, qk_ref, v_ref, lr_ref, u_ref, ub_ref, tri_ref, cm_ref, mkk_ref, mkv_ref,
               mkvf_ref, s_ref, fwd):
    L = GLA_CHUNK
    r0 = pl.multiple_of(c * L, L)
    qk = qk_ref[pl.ds(r0, L), :]
    q = qk[:, :GLA_KEY].astype(f32)
    k = qk[:, GLA_KEY:].astype(f32)
    v = v_ref[pl.ds(r0, L), :]
    z = _dot(lr_ref[pl.ds(r0, L), :], u_ref[...]) + ub_ref[...]
    la = (jnp.minimum(z, 0.0) - jnp.log1p(jnp.exp(-jnp.abs(z)))) * (1.0 / GLA_TAU)
    hi = la.astype(bf16)
    lo = (la - hi.astype(f32)).astype(bf16)
    tri = tri_ref[...]
    b = _dot(tri, hi) + _dot(tri, lo)
    b_end = b[L - 1:L, :] if fwd else b[0:1, :]
    q_dec = (q * (GLA_DK ** -0.5) * jnp.exp(b)).astype(bf16)
    k_inv = (k * jnp.exp(-b)).astype(bf16)
    k_end = (k * jnp.exp(b_end - b)).astype(bf16)
    kb = jnp.concatenate([k_inv] * GLA_HEADS, axis=0) * mkk_ref[...]
    att = _dot(q_dec, kb, _NT) * cm_ref[...]
    vb = jnp.concatenate([v] * GLA_HEADS, axis=0) * mkv_ref[...]
    s_old = s_ref[...]
    o = _dot(att.astype(bf16), vb) + _dot(q_dec, s_old.astype(bf16))
    ds = _dot(k_end, v, _TN)
    ones = jnp.ones((L, GLA_DV), bf16)
    tot = _dot(hi, ones, _TN) + _dot(lo, ones, _TN)
    dec = jnp.exp(tot)
    s_ref[...] = jnp.concatenate([dec] * GLA_HEADS, axis=1) * s_old + ds * mkvf_ref[...]
    return o


def _gla_kernel(qk_ref, v_ref, lr_ref, uf_ref, ubf_ref, ub_ref, ubb_ref, ng_ref,
                tril_ref, triu_ref, cml_ref, cmu_ref, mkk_ref, mkv_ref, mkvf_ref,
                o_ref, acc_ref, sf_ref, sb_ref, *, n_chunks):
    L = GLA_CHUNK
    sf_ref[...] = jnp.zeros_like(sf_ref)
    sb_ref[...] = jnp.zeros_like(sb_ref)
    shared = (mkk_ref, mkv_ref, mkvf_ref)

    def both(i):
        cf, cb = i, n_chunks - 1 - i
        of = _gla_chunk(cf, qk_ref, v_ref, lr_ref, uf_ref, ubf_ref, tril_ref, cml_ref, *shared,
                        sf_ref, True)
        ob = _gla_chunk(cb, qk_ref, v_ref, lr_ref, ub_ref, ubb_ref, triu_ref, cmu_ref, *shared,
                        sb_ref, False)
        return cf, cb, of, ob

    def first_half(i, carry):
        cf, cb, of, ob = both(i)
        acc_ref[pl.ds(pl.multiple_of(cf * L, L), L), :] = of
        acc_ref[pl.ds(pl.multiple_of(cb * L, L), L), :] = ob
        return carry

    def finish(c, o_new):
        r0 = pl.multiple_of(c * L, L)
        o = acc_ref[pl.ds(r0, L), :] + o_new
        outs = []
        for h in range(GLA_HEADS):
            oh = o[:, h * GLA_DV:(h + 1) * GLA_DV]
            ms = jnp.mean(oh * oh, axis=-1, keepdims=True)
            outs.append(oh * lax.rsqrt(ms + RMS_EPS) * ng_ref[...])
        o_ref[pl.ds(r0, L), :] = jnp.concatenate(outs, axis=1).astype(o_ref.dtype)

    def second_half(i, carry):
        cf, cb, of, ob = both(i)
        finish(cf, of)
        finish(cb, ob)
        return carry

    lax.fori_loop(0, n_chunks // 2, first_half, 0)
    lax.fori_loop(n_chunks // 2, n_chunks, second_half, 0)


def _gla_constants():
    L, H = GLA_CHUNK, GLA_HEADS
    i = np.arange(L)
    tril = (i[None, :] <= i[:, None]).astype(np.float32)
    triu = tril.T.copy()
    hk = np.arange(H * L) // L
    hv = np.arange(H * GLA_DV) // GLA_DV
    mkk = (hk[:, None] == hk[None, :]).astype(np.float32)
    mkv = (hk[:, None] == hv[None, :]).astype(np.float32)
    return dict(
        tril=jnp.asarray(tril, bf16), triu=jnp.asarray(triu, bf16),
        cml=jnp.asarray(np.tile(tril, (1, H)), f32), cmu=jnp.asarray(np.tile(triu, (1, H)), f32),
        mkk=jnp.asarray(mkk, bf16), mkv=jnp.asarray(mkv, bf16), mkvf=jnp.asarray(mkv, f32))


def _gla(qk, v, lr, uf, ubf, ub, ubb, ng, cst):
    B, T, _ = qk.shape
    assert T % (2 * GLA_CHUNK) == 0
    seq = lambda w: pl.BlockSpec((None, T, w), lambda b: (b, 0, 0))
    consts = (uf, ubf, ub, ubb, ng, cst["tril"], cst["triu"], cst["cml"], cst["cmu"],
              cst["mkk"], cst["mkv"], cst["mkvf"])
    return pl.pallas_call(
        functools.partial(_gla_kernel, n_chunks=T // GLA_CHUNK),
        grid=(B,),
        in_specs=[seq(2 * GLA_KEY), seq(GLA_VAL), seq(LR_PAD)] + [_const_spec(c.shape) for c in consts],
        out_specs=seq(GLA_VAL),
        out_shape=jax.ShapeDtypeStruct((B, T, GLA_VAL), bf16),
        scratch_shapes=[pltpu.VMEM((T, GLA_VAL), f32),
                        pltpu.VMEM((GLA_KEY, GLA_VAL), f32),
                        pltpu.VMEM((GLA_KEY, GLA_VAL), f32)],
        compiler_params=_params(("parallel",)),
        name="gla",
    )(qk, v, lr, *consts)


def _napool_kernel(qkv_ref, p_ref, bias_ref, mkk_ref, hm_ref, a_ref, half_ref, pw_ref, ps_ref,
                   yc_ref, yb_ref, *, rows, seq_len):
    W = GRID_W
    wr = NA_WIN_R

    def body(r, carry):
        t0 = pl.multiple_of(r * W, W)
        r0 = jnp.clip(r - wr // 2, 0, rows - wr)
        k0 = pl.multiple_of(r0 * W, W)
        q = qkv_ref[pl.ds(t0, W), 0:NA_WIDTH]
        kw = qkv_ref[pl.ds(k0, wr * W), NA_WIDTH:2 * NA_WIDTH]
        vw = qkv_ref[pl.ds(k0, wr * W), 2 * NA_WIDTH:3 * NA_WIDTH]
        qs = jnp.concatenate([q] * NA_HEADS, axis=0) * mkk_ref[...]
        s = _dot(qs, kw, _NT) * (NA_HD ** -0.5) + bias_ref[r - r0]
        m = jnp.max(s, axis=-1, keepdims=True)
        e = jnp.exp(s - m)
        den = jnp.sum(e, axis=-1, keepdims=True)
        pv = _dot(e.astype(bf16), vw) / den
        hm = hm_ref[...]
        out = pv[0:W] * hm[0:1]
        for h in range(1, NA_HEADS):
            out = out + pv[h * W:(h + 1) * W] * hm[h:h + 1]
        yc_ref[pl.ds(t0, W), :] = out.astype(yc_ref.dtype)

        pc = p_ref[pl.ds(t0, W), :]
        tp = pl.multiple_of(jnp.maximum(r - 1, 0) * W, W)
        tn = pl.multiple_of(jnp.minimum(r + 1, rows - 1) * W, W)
        pp = p_ref[pl.ds(tp, W), :]
        pn = p_ref[pl.ds(tn, W), :]
        pp = jnp.where(r > 0, pp, jnp.zeros_like(pp))
        pn = jnp.where(r < rows - 1, pn, jnp.zeros_like(pn))
        p3 = jnp.concatenate([pp, pc, pn], axis=0)
        wsum = _dot(a_ref[0], p3) * hm[0:1]
        for g in range(1, POOL_GROUPS):
            wsum = wsum + _dot(a_ref[g], p3) * hm[g:g + 1]
        t = t0 + lax.broadcasted_iota(jnp.int32, (W, POOL_WIDTH), 0)
        half = half_ref[...]
        cnt = (jnp.minimum(t + half, seq_len) - jnp.maximum(t - half, 0)).astype(f32)
        d = (wsum / cnt - pc.astype(f32)).astype(bf16)
        yb_ref[pl.ds(t0, W), :] = (_dot(d, pw_ref[...]) * ps_ref[...]).astype(yb_ref.dtype)
        return carry

    lax.fori_loop(0, rows, body, 0)


def _napool_constants():
    W = GRID_W
    hk = np.arange(NA_WIDTH) // NA_HD
    hm = (np.arange(NA_HEADS)[:, None] == hk[None, :]).astype(np.float32)
    a = np.zeros((POOL_GROUPS, W, 3 * W), np.float32)
    t = np.arange(W)[:, None]
    j = np.arange(3 * W)[None, :] - W
    for g, w in enumerate(POOL_WINDOWS):
        a[g] = ((j >= t - w // 2) & (j < t + w - w // 2)).astype(np.float32)
    half = np.repeat(np.array([w // 2 for w in POOL_WINDOWS], np.int32), POOL_GC)[None, :]
    return dict(hm=jnp.asarray(hm, f32), a=jnp.asarray(a, bf16), half=jnp.asarray(half, jnp.int32))


def _na_bias_table(rpb):
    W, R, C = GRID_W, NA_WIN_R, NA_WIN_C
    cols = np.arange(W)
    c0 = np.clip(cols - C // 2, 0, W - C)
    delta = np.arange(R)[:, None]
    kr = np.arange(R)[None, :]
    row_off = kr - delta + (R - 1)
    col_off = cols[None, :] - cols[:, None] + (C - 1)
    valid = (cols[None, :] >= c0[:, None]) & (cols[None, :] < c0[:, None] + C)
    col_off = np.clip(col_off, 0, 2 * C - 2)
    tab = rpb[:, row_off][:, :, :, col_off]
    tab = jnp.where(jnp.asarray(valid)[None, None, None], tab, NEG_BIG)
    tab = jnp.transpose(tab, (1, 0, 3, 2, 4))
    return tab.reshape(R, NA_HEADS * W, R * W).astype(f32)


def _napool(qkv, p, bias, mkk, pw, ps, cst):
    B, T, _ = qkv.shape
    rows = T // GRID_W
    assert rows >= NA_WIN_R
    seq = lambda w: pl.BlockSpec((None, T, w), lambda b: (b, 0, 0))
    consts = (bias, mkk, cst["hm"], cst["a"], cst["half"], pw, ps)
    return pl.pallas_call(
        functools.partial(_napool_kernel, rows=rows, seq_len=T),
        grid=(B,),
        in_specs=[seq(3 * NA_WIDTH), seq(POOL_WIDTH)] + [_const_spec(c.shape) for c in consts],
        out_specs=[seq(NA_WIDTH), seq(POOL_WIDTH)],
        out_shape=[jax.ShapeDtypeStruct((B, T, NA_WIDTH), bf16),
                   jax.ShapeDtypeStruct((B, T, POOL_WIDTH), bf16)],
        compiler_params=_params(("parallel",)),
        name="napool",
    )(qkv, p, *consts)


def _merge_kernel(x_ref, on_ref, og_ref, yb_ref, yc_ref, g_ref, wa_ref, wb_ref, wc_ref, wo_ref,
                  lg_ref, lb_ref, o_ref):
    og = og_ref[...].astype(f32)
    a_in = (on_ref[...].astype(f32) * (og * _sigmoid(og))).astype(bf16)
    ya = _dot(a_in, wa_ref[...])
    yb = _dot(yb_ref[...], wb_ref[...])
    yc = _dot(yc_ref[...], wc_ref[...])
    D = D_MODEL
    m = (_sigmoid(g_ref[:, 0:D].astype(f32)) * ya
         + _sigmoid(g_ref[:, D:2 * D].astype(f32)) * yb
         + _sigmoid(g_ref[:, 2 * D:3 * D].astype(f32)) * yc)
    y = DEEPNORM_ALPHA * x_ref[...] + _dot(m.astype(bf16), wo_ref[...])
    o_ref[...] = _layer_norm(y, lg_ref[...], lb_ref[...])


def _merge(x, on, og, yb, yc, g, wa, wb, wc, wo, lg, lb, tm):
    B, T, D = x.shape
    tok = lambda w: pl.BlockSpec((None, tm, w), lambda b, i: (b, i, 0))
    consts = (wa, wb, wc, wo, lg, lb)
    return pl.pallas_call(
        _merge_kernel,
        grid=(B, T // tm),
        in_specs=[tok(D), tok(GLA_VAL), tok(GLA_VAL), tok(POOL_WIDTH), tok(NA_WIDTH),
                  tok(N_BRANCH * D)] + [_const_spec(c.shape) for c in consts],
        out_specs=tok(D),
        out_shape=jax.ShapeDtypeStruct((B, T, D), f32),
        compiler_params=_params(("parallel", "parallel")),
        name="merge",
    )(x, on, og, yb, yc, g, *consts)


def _ffn_kernel(x_ref, wg_ref, wu_ref, wd_ref, lg_ref, lb_ref, o_ref):
    x = x_ref[...]
    xb = x.astype(bf16)
    hg = _dot(xb, wg_ref[...])
    hu = _dot(xb, wu_ref[...])
    act = (hg * _sigmoid(hg) * hu).astype(bf16)
    y = DEEPNORM_ALPHA * x + _dot(act, wd_ref[...])
    o_ref[...] = _layer_norm(y, lg_ref[...], lb_ref[...])


def _ffn(x, wg, wu, wd, lg, lb, tm):
    B, T, D = x.shape
    tok = pl.BlockSpec((None, tm, D), lambda b, i: (b, i, 0))
    consts = (wg, wu, wd, lg, lb)
    return pl.pallas_call(
        _ffn_kernel,
        grid=(B, T // tm),
        in_specs=[tok] + [_const_spec(c.shape) for c in consts],
        out_specs=tok,
        out_shape=jax.ShapeDtypeStruct((B, T, D), f32),
        compiler_params=_params(("parallel", "parallel")),
        name="ffn",
    )(x, *consts)


def _prepare_layer(l, w_in, gla_up_f, gla_up_b, gla_bias_f, gla_bias_b, gla_norm, pool_w, pool_scale,
                   na_rpb, w_br_a, w_br_b, w_br_c, w_out, ln1_g, ln1_b, w_gate, w_up, w_down,
                   ln2_g, ln2_b):
    wi = w_in[l]
    o_lr = 2 * GLA_KEY + 2 * GLA_VAL
    o_p = o_lr + 2 * GLA_RANK
    o_g = o_p + POOL_WIDTH + 3 * NA_WIDTH
    wlr = jnp.zeros((D_MODEL, LR_PAD), f32).at[:, :2 * GLA_RANK].set(wi[:, o_lr:o_p])
    uf = jnp.zeros((LR_PAD, GLA_KEY), f32).at[:GLA_RANK].set(gla_up_f[l])
    ub = jnp.zeros((LR_PAD, GLA_KEY), f32).at[GLA_RANK:2 * GLA_RANK].set(gla_up_b[l])
    row = lambda a: a.reshape(1, -1).astype(f32)
    return dict(
        wa=wi[:, :o_lr].astype(bf16), wlr=wlr.astype(bf16), wb=wi[:, o_p:o_g].astype(bf16),
        wg=wi[:, o_g:].astype(bf16),
        uf=uf.astype(bf16), ub=ub.astype(bf16), ubf=row(gla_bias_f[l]), ubb=row(gla_bias_b[l]),
        ng=row(gla_norm[l]),
        pw=jax.scipy.linalg.block_diag(*[pool_w[l, g] for g in range(POOL_GROUPS)]).astype(bf16),
        ps=row(pool_scale[l]), bias=_na_bias_table(na_rpb[l]),
        bra=w_br_a[l].astype(bf16), brb=w_br_b[l].astype(bf16), brc=w_br_c[l].astype(bf16),
        wo=w_out[l].astype(bf16), ln1g=row(ln1_g[l]), ln1b=row(ln1_b[l]),
        wgate=w_gate[l].astype(bf16), wup=w_up[l].astype(bf16), wdown=w_down[l].astype(bf16),
        ln2g=row(ln2_g[l]), ln2b=row(ln2_b[l]))


def _trunk(x, layers, gla_cst, np_cst, tm):
    for p in layers:
        qk, v, og, lr, pin, qkv, gates = _inproj(x, p["wa"], p["wlr"], p["wb"], p["wg"], tm)
        on = _gla(qk, v, lr, p["uf"], p["ubf"], p["ub"], p["ubb"], p["ng"], gla_cst)
        yc, yb = _napool(qkv, pin, p["bias"], gla_cst["mkk"], p["pw"], p["ps"], np_cst)
        x = _merge(x, on, og, yb, yc, gates, p["bra"], p["brb"], p["brc"], p["wo"],
                   p["ln1g"], p["ln1b"], tm)
        x = _ffn(x, p["wgate"], p["wup"], p["wdown"], p["ln2g"], p["ln2b"], tm)
    return x


def kernel(x_prompt, x_sample, w_in, gla_up_f, gla_up_b, gla_bias_f, gla_bias_b, gla_norm, pool_w, pool_scale, na_rpb, w_br_a, w_br_b, w_br_c, w_out, ln1_g, ln1_b, w_gate, w_up, w_down, ln2_g, ln2_b):
    weights = (w_in, gla_up_f, gla_up_b, gla_bias_f, gla_bias_b, gla_norm, pool_w, pool_scale, na_rpb,
               w_br_a, w_br_b, w_br_c, w_out, ln1_g, ln1_b, w_gate, w_up, w_down, ln2_g, ln2_b)
    layers = [_prepare_layer(l, *weights) for l in range(w_in.shape[0])]
    gla_cst = _gla_constants()
    np_cst = _napool_constants()
    tm = 512
    return (_trunk(x_prompt, layers, gla_cst, np_cst, tm), _trunk(x_sample, layers, gla_cst, np_cst, tm))
```

```python
import functools

import numpy as np
import jax
import jax.numpy as jnp
from jax import lax
from jax.experimental import pallas as pl
from jax.experimental.pallas import tpu as pltpu

f32 = jnp.float32
bf16 = jnp.bfloat16

D_MODEL = 1024
DEPTH = 4
GRID_W = 64
GLA_HEADS = 4
GLA_DK = 64
GLA_DV = 128
GLA_KEY = GLA_HEADS * GLA_DK
GLA_VAL = GLA_HEADS * GLA_DV
GLA_RANK = 16
GLA_TAU = 16.0
GLA_CHUNK = 64
GLA_SUPER = 4
POOL_GROUPS = 4
POOL_GC = 64
POOL_WIDTH = POOL_GROUPS * POOL_GC
POOL_WINDOWS = (2, 4, 8, 16)
NA_HEADS = 4
NA_HD = 64
NA_WIDTH = NA_HEADS * NA_HD
NA_WIN_R = 8
NA_WIN_C = 16
N_BRANCH = 3
D_FF = 2816
DEEPNORM_ALPHA = (2 * DEPTH) ** 0.25
LN_EPS = 1e-5
RMS_EPS = 1e-6
LR_PAD = 128
NEG_BIG = -1e30

VMEM_LIMIT_BYTES = 56 * 1024 * 1024

_NN = (((1,), (0,)), ((), ()))
_NT = (((1,), (1,)), ((), ()))
_TN = (((0,), (0,)), ((), ()))


def _dot(a, b, dims=_NN):
    return lax.dot_general(a, b, dims, preferred_element_type=f32)


def _const_spec(shape):
    nd = len(shape)
    return pl.BlockSpec(shape, lambda *_: (0,) * nd, pipeline_mode=pl.Buffered(1))


def _params(sem):
    return pltpu.CompilerParams(dimension_semantics=sem, vmem_limit_bytes=VMEM_LIMIT_BYTES)


def _layer_norm(y, g, b):
    mu = jnp.mean(y, axis=-1, keepdims=True)
    yc = y - mu
    var = jnp.mean(yc * yc, axis=-1, keepdims=True)
    return yc * lax.rsqrt(var + LN_EPS) * g + b


def _silu(x):
    h = 0.5 * x
    return h * jnp.tanh(h) + h


def _inproj_kernel(x_ref, wa_ref, wlr_ref, wb_ref, wg_ref,
                   qk_ref, v_ref, og_ref, lr_ref, p_ref, qkv_ref, g_ref):
    xb = x_ref[...].astype(bf16)
    ra = _dot(xb, wa_ref[...])
    qk_ref[...] = ra[:, :2 * GLA_KEY].astype(bf16)
    v_ref[...] = ra[:, 2 * GLA_KEY:2 * GLA_KEY + GLA_VAL].astype(bf16)
    og_ref[...] = ra[:, 2 * GLA_KEY + GLA_VAL:].astype(bf16)
    lr_ref[...] = _dot(xb, wlr_ref[...]).astype(bf16)
    rb = _dot(xb, wb_ref[...])
    p_ref[...] = rb[:, :POOL_WIDTH].astype(bf16)
    qkv_ref[...] = rb[:, POOL_WIDTH:].astype(bf16)
    g_ref[...] = _dot(xb, wg_ref[...]).astype(bf16)


def _inproj(x, wa, wlr, wb, wg, tm):
    B, T, D = x.shape
    widths = (2 * GLA_KEY, GLA_VAL, GLA_VAL, LR_PAD, POOL_WIDTH, 3 * NA_WIDTH, N_BRANCH * D_MODEL)
    tok = lambda w: pl.BlockSpec((None, tm, w), lambda b, i: (b, i, 0))
    return pl.pallas_call(
        _inproj_kernel,
        grid=(B, T // tm),
        in_specs=[tok(D), _const_spec(wa.shape), _const_spec(wlr.shape), _const_spec(wb.shape),
                  _const_spec(wg.shape)],
        out_specs=[tok(w) for w in widths],
        out_shape=[jax.ShapeDtypeStruct((B, T, w), bf16) for w in widths],
        compiler_params=_params(("parallel", "parallel")),
        name="inproj",
    )(x, wa, wlr, wb, wg)


def _gla_super(sc, qk_ref, v_ref, lr_ref, u_ref, ub_ref, tri_ref, cm_ref, mkk_ref, mkv_ref, hm_ref,
               st_ref, fwd):
    L, NS, H = GLA_CHUNK, GLA_SUPER, GLA_HEADS
    R = L * NS
    r0 = pl.multiple_of(sc * R, R)
    qk = qk_ref[pl.ds(r0, R), :]
    q = qk[:, :GLA_KEY].astype(f32)
    k = qk[:, GLA_KEY:].astype(f32)
    v = v_ref[pl.ds(r0, R), :]
    z = _dot(lr_ref[pl.ds(r0, R), :], u_ref[...]) + ub_ref[...]
    la = (jnp.minimum(z, 0.0) - jnp.log1p(jnp.exp(-jnp.abs(z)))) * (1.0 / GLA_TAU)
    hi = la.astype(bf16)
    lo = (la - hi.astype(f32)).astype(bf16)
    tri = tri_ref[...]
    b = _dot(tri, hi) + _dot(tri, lo)
    ends = [b[c * L + L - 1:c * L + L, :] if fwd else b[c * L:c * L + 1, :] for c in range(NS)]
    b_end = jnp.concatenate([jnp.broadcast_to(e, (L, GLA_KEY)) for e in ends], axis=0)
    q_dec = (q * (GLA_DK ** -0.5) * jnp.exp(b)).astype(bf16)
    k_inv = (k * jnp.exp(-b)).astype(bf16)
    k_end = (k * jnp.exp(b_end - b)).astype(bf16)
    hm = hm_ref[...]
    state = [st_ref[p] for p in range(H // 2)]
    outs = [None] * NS
    for c in (range(NS) if fwd else range(NS - 1, -1, -1)):
        rows = slice(c * L, (c + 1) * L)
        qc, kic, kec, vc = q_dec[rows], k_inv[rows], k_end[rows], v[rows]
        kb = jnp.concatenate([kic] * H, axis=0) * mkk_ref[...]
        att = _dot(qc, kb, _NT) * cm_ref[...]
        vb = jnp.concatenate([vc] * H, axis=0) * mkv_ref[...]
        o = _dot(att.astype(bf16), vb)
        inter = []
        dec = jnp.exp(ends[c])
        for p in range(H // 2):
            lanes = slice(p * 2 * GLA_DK, (p + 1) * 2 * GLA_DK)
            q2 = jnp.concatenate([qc[:, lanes] * hm[0:1], qc[:, lanes] * hm[1:2]], axis=0)
            oi = _dot(q2, state[p].astype(bf16), _NT)
            inter += [oi[0:L], oi[L:2 * L]]
            ke2 = jnp.concatenate([kec[:, lanes] * hm[0:1], kec[:, lanes] * hm[1:2]], axis=0)
            v2 = jnp.concatenate([vc[:, (2 * p) * GLA_DV:(2 * p + 1) * GLA_DV],
                                  vc[:, (2 * p + 1) * GLA_DV:(2 * p + 2) * GLA_DV]], axis=0)
            state[p] = dec[:, lanes] * state[p] + _dot(v2, ke2, _TN)
        outs[c] = o + jnp.concatenate(inter, axis=1)
    for p in range(H // 2):
        st_ref[p] = state[p]
    return jnp.concatenate(outs, axis=0)


def _gla_kernel(qk_ref, v_ref, lr_ref, uf_ref, ubf_ref, ub_ref, ubb_ref, ng_ref,
                tril_ref, triu_ref, cml_ref, cmu_ref, mkk_ref, mkv_ref, hm_ref,
                o_ref, acc_ref, sf_ref, sb_ref, *, n_super):
    R = GLA_CHUNK * GLA_SUPER
    sf_ref[...] = jnp.zeros_like(sf_ref)
    sb_ref[...] = jnp.zeros_like(sb_ref)
    shared = (mkk_ref, mkv_ref, hm_ref)

    def both(i):
        cf, cb = i, n_super - 1 - i
        of = _gla_super(cf, qk_ref, v_ref, lr_ref, uf_ref, ubf_ref, tril_ref, cml_ref, *shared,
                        sf_ref, True)
        ob = _gla_super(cb, qk_ref, v_ref, lr_ref, ub_ref, ubb_ref, triu_ref, cmu_ref, *shared,
                        sb_ref, False)
        return cf, cb, of, ob

    def first_half(i, carry):
        cf, cb, of, ob = both(i)
        acc_ref[pl.ds(pl.multiple_of(cf * R, R), R), :] = of
        acc_ref[pl.ds(pl.multiple_of(cb * R, R), R), :] = ob
        return carry

    def finish(c, o_new):
        r0 = pl.multiple_of(c * R, R)
        o = acc_ref[pl.ds(r0, R), :] + o_new
        outs = []
        for h in range(GLA_HEADS):
            oh = o[:, h * GLA_DV:(h + 1) * GLA_DV]
            ms = jnp.mean(oh * oh, axis=-1, keepdims=True)
            outs.append(oh * lax.rsqrt(ms + RMS_EPS) * ng_ref[...])
        o_ref[pl.ds(r0, R), :] = jnp.concatenate(outs, axis=1).astype(o_ref.dtype)

    def second_half(i, carry):
        cf, cb, of, ob = both(i)
        finish(cf, of)
        finish(cb, ob)
        return carry

    lax.fori_loop(0, n_super // 2, first_half, 0)
    lax.fori_loop(n_super // 2, n_super, second_half, 0)


def _gla_constants():
    L, H, NS = GLA_CHUNK, GLA_HEADS, GLA_SUPER
    i = np.arange(L)
    tril = (i[None, :] <= i[:, None]).astype(np.float32)
    triu = tril.T.copy()
    eye = np.eye(NS, dtype=np.float32)
    hk = np.arange(H * L) // L
    hv = np.arange(H * GLA_DV) // GLA_DV
    mkk = (hk[:, None] == hk[None, :]).astype(np.float32)
    mkv = (hk[:, None] == hv[None, :]).astype(np.float32)
    half = (np.arange(2)[:, None] == (np.arange(2 * GLA_DK) // GLA_DK)[None, :]).astype(np.float32)
    return dict(
        tril=jnp.asarray(np.kron(eye, tril), bf16), triu=jnp.asarray(np.kron(eye, triu), bf16),
        cml=jnp.asarray(np.tile(tril, (1, H)), f32), cmu=jnp.asarray(np.tile(triu, (1, H)), f32),
        mkk=jnp.asarray(mkk, bf16), mkv=jnp.asarray(mkv, bf16), hm=jnp.asarray(half, bf16))


def _gla(qk, v, lr, uf, ubf, ub, ubb, ng, cst):
    B, T, _ = qk.shape
    R = GLA_CHUNK * GLA_SUPER
    assert T % (2 * R) == 0
    seq = lambda w: pl.BlockSpec((None, T, w), lambda b: (b, 0, 0))
    consts = (uf, ubf, ub, ubb, ng, cst["tril"], cst["triu"], cst["cml"], cst["cmu"],
              cst["mkk"], cst["mkv"], cst["hm"])
    state = pltpu.VMEM((GLA_HEADS // 2, GLA_DV, 2 * GLA_DK), f32)
    return pl.pallas_call(
        functools.partial(_gla_kernel, n_super=T // R),
        grid=(B,),
        in_specs=[seq(2 * GLA_KEY), seq(GLA_VAL), seq(LR_PAD)] + [_const_spec(c.shape) for c in consts],
        out_specs=seq(GLA_VAL),
        out_shape=jax.ShapeDtypeStruct((B, T, GLA_VAL), bf16),
        scratch_shapes=[pltpu.VMEM((T, GLA_VAL), f32), state, state],
        compiler_params=_params(("parallel",)),
        name="gla",
    )(qk, v, lr, *consts)


def _napool_kernel(qkv_ref, p_ref, bias_ref, mkk_ref, hm_ref, a_ref, half_ref, pw_ref, ps_ref,
                   yc_ref, yb_ref, s0_ref, s1_ref, e0_ref, e1_ref, d0_ref, d1_ref, *, rows, seq_len):
    W = GRID_W
    wr = NA_WIN_R

    def window(r):
        r0 = jnp.clip(r - wr // 2, 0, rows - wr)
        return r0, pl.multiple_of(r0 * W, W)

    def scores(r, s_ref):
        r = jnp.minimum(r, rows - 1)
        t0 = pl.multiple_of(r * W, W)
        r0, k0 = window(r)
        q = qkv_ref[pl.ds(t0, W), 0:NA_WIDTH] * (NA_HD ** -0.5)
        kw = qkv_ref[pl.ds(k0, wr * W), NA_WIDTH:2 * NA_WIDTH]
        qs = jnp.concatenate([q] * NA_HEADS, axis=0) * mkk_ref[...]
        s_ref[...] = _dot(qs, kw, _NT) + bias_ref[r - r0]

    def softmax(s_ref, e_ref, d_ref):
        s = s_ref[...]
        e = jnp.exp(s - jnp.max(s, axis=-1, keepdims=True))
        d_ref[...] = jnp.sum(e, axis=-1, keepdims=True)
        e_ref[...] = e.astype(bf16)

    def finish(r, e_ref, d_ref):
        t0 = pl.multiple_of(r * W, W)
        _, k0 = window(r)
        vw = qkv_ref[pl.ds(k0, wr * W), 2 * NA_WIDTH:3 * NA_WIDTH]
        pv = _dot(e_ref[...], vw) / d_ref[...]
        hm = hm_ref[...]
        out = pv[0:W] * hm[0:1]
        for h in range(1, NA_HEADS):
            out = out + pv[h * W:(h + 1) * W] * hm[h:h + 1]
        yc_ref[pl.ds(t0, W), :] = out.astype(yc_ref.dtype)

        pc = p_ref[pl.ds(t0, W), :]
        tp = pl.multiple_of(jnp.maximum(r - 1, 0) * W, W)
        tn = pl.multiple_of(jnp.minimum(r + 1, rows - 1) * W, W)
        pp = p_ref[pl.ds(tp, W), :]
        pn = p_ref[pl.ds(tn, W), :]
        pp = jnp.where(r > 0, pp, jnp.zeros_like(pp))
        pn = jnp.where(r < rows - 1, pn, jnp.zeros_like(pn))
        p3 = jnp.concatenate([pp, pc, pn], axis=0)
        wsum = _dot(a_ref[0], p3) * hm[0:1]
        for g in range(1, POOL_GROUPS):
            wsum = wsum + _dot(a_ref[g], p3) * hm[g:g + 1]
        t = t0 + lax.broadcasted_iota(jnp.int32, (W, POOL_WIDTH), 0)
        half = half_ref[...]
        cnt = (jnp.minimum(t + half, seq_len) - jnp.maximum(t - half, 0)).astype(f32)
        d = (wsum / cnt - pc.astype(f32)).astype(bf16)
        yb_ref[pl.ds(t0, W), :] = (_dot(d, pw_ref[...]) * ps_ref[...]).astype(yb_ref.dtype)

    scores(0, s0_ref)
    softmax(s0_ref, e0_ref, d0_ref)
    scores(1, s1_ref)

    def body(j, carry):
        r = 2 * j
        finish(r, e0_ref, d0_ref)
        softmax(s1_ref, e1_ref, d1_ref)
        scores(r + 2, s0_ref)
        finish(r + 1, e1_ref, d1_ref)
        softmax(s0_ref, e0_ref, d0_ref)
        scores(r + 3, s1_ref)
        return carry

    lax.fori_loop(0, rows // 2, body, 0)


def _napool_constants():
    W = GRID_W
    hk = np.arange(NA_WIDTH) // NA_HD
    hm = (np.arange(NA_HEADS)[:, None] == hk[None, :]).astype(np.float32)
    a = np.zeros((POOL_GROUPS, W, 3 * W), np.float32)
    t = np.arange(W)[:, None]
    j = np.arange(3 * W)[None, :] - W
    for g, w in enumerate(POOL_WINDOWS):
        a[g] = ((j >= t - w // 2) & (j < t + w - w // 2)).astype(np.float32)
    half = np.repeat(np.array([w // 2 for w in POOL_WINDOWS], np.int32), POOL_GC)[None, :]
    return dict(hm=jnp.asarray(hm, f32), a=jnp.asarray(a, bf16), half=jnp.asarray(half, jnp.int32))


def _na_bias_table(rpb):
    W, R, C = GRID_W, NA_WIN_R, NA_WIN_C
    cols = np.arange(W)
    c0 = np.clip(cols - C // 2, 0, W - C)
    delta = np.arange(R)[:, None]
    kr = np.arange(R)[None, :]
    row_off = kr - delta + (R - 1)
    col_off = cols[None, :] - cols[:, None] + (C - 1)
    valid = (cols[None, :] >= c0[:, None]) & (cols[None, :] < c0[:, None] + C)
    col_off = np.clip(col_off, 0, 2 * C - 2)
    tab = rpb[:, row_off][:, :, :, col_off]
    tab = jnp.where(jnp.asarray(valid)[None, None, None], tab, NEG_BIG)
    tab = jnp.transpose(tab, (1, 0, 3, 2, 4))
    return tab.reshape(R, NA_HEADS * W, R * W).astype(f32)


def _napool(qkv, p, bias, mkk, pw, ps, cst):
    B, T, _ = qkv.shape
    rows = T // GRID_W
    assert rows >= NA_WIN_R and rows % 2 == 0
    seq = lambda w: pl.BlockSpec((None, T, w), lambda b: (b, 0, 0))
    consts = (bias, mkk, cst["hm"], cst["a"], cst["half"], pw, ps)
    return pl.pallas_call(
        functools.partial(_napool_kernel, rows=rows, seq_len=T),
        grid=(B,),
        in_specs=[seq(3 * NA_WIDTH), seq(POOL_WIDTH)] + [_const_spec(c.shape) for c in consts],
        out_specs=[seq(NA_WIDTH), seq(POOL_WIDTH)],
        out_shape=[jax.ShapeDtypeStruct((B, T, NA_WIDTH), bf16),
                   jax.ShapeDtypeStruct((B, T, POOL_WIDTH), bf16)],
        scratch_shapes=[pltpu.VMEM((NA_HEADS * GRID_W, NA_WIN_R * GRID_W), f32)] * 2
        + [pltpu.VMEM((NA_HEADS * GRID_W, NA_WIN_R * GRID_W), bf16)] * 2
        + [pltpu.VMEM((NA_HEADS * GRID_W, 1), f32)] * 2,
        compiler_params=_params(("parallel",)),
        name="napool",
    )(qkv, p, *consts)


def _merge_kernel(x_ref, on_ref, og_ref, yb_ref, yc_ref, g_ref, wa_ref, wb_ref, wc_ref, wo_ref,
                  lg_ref, lb_ref, o_ref):
    D = D_MODEL
    one = jnp.ones((), bf16)
    tm = x_ref.shape[0]
    for rows in (pl.ds(0, tm // 2), pl.ds(tm // 2, tm // 2)):
        h = og_ref[rows, :]
        a_in = on_ref[rows, :] * (h * jnp.tanh(h) + h)
        m = ((one + jnp.tanh(g_ref[rows, 0:D])) * _dot(a_in, wa_ref[...]).astype(bf16)
             + (one + jnp.tanh(g_ref[rows, D:2 * D])) * _dot(yb_ref[rows, :], wb_ref[...]).astype(bf16)
             + (one + jnp.tanh(g_ref[rows, 2 * D:3 * D])) * _dot(yc_ref[rows, :], wc_ref[...]).astype(bf16))
        y = DEEPNORM_ALPHA * x_ref[rows, :] + _dot(m, wo_ref[...])
        o_ref[rows, :] = _layer_norm(y, lg_ref[...], lb_ref[...])


def _merge(x, on, og, yb, yc, g, wa, wb, wc, wo, lg, lb, tm):
    B, T, D = x.shape
    tok = lambda w: pl.BlockSpec((None, tm, w), lambda b, i: (b, i, 0))
    consts = (wa, wb, wc, wo, lg, lb)
    return pl.pallas_call(
        _merge_kernel,
        grid=(B, T // tm),
        in_specs=[tok(D), tok(GLA_VAL), tok(GLA_VAL), tok(POOL_WIDTH), tok(NA_WIDTH),
                  tok(N_BRANCH * D)] + [_const_spec(c.shape) for c in consts],
        out_specs=tok(D),
        out_shape=jax.ShapeDtypeStruct((B, T, D), f32),
        compiler_params=_params(("parallel", "parallel")),
        name="merge",
    )(x, on, og, yb, yc, g, *consts)


def _ffn_kernel(x_ref, wg_ref, wu_ref, wd_ref, lg_ref, lb_ref, o_ref):
    tm = x_ref.shape[0]
    for rows in (pl.ds(0, tm // 2), pl.ds(tm // 2, tm // 2)):
        x = x_ref[rows, :]
        xb = x.astype(bf16)
        hg = _dot(xb, wg_ref[...])
        hu = _dot(xb, wu_ref[...])
        act = (_silu(hg) * hu).astype(bf16)
        y = DEEPNORM_ALPHA * x + _dot(act, wd_ref[...])
        o_ref[rows, :] = _layer_norm(y, lg_ref[...], lb_ref[...])


def _ffn(x, wg, wu, wd, lg, lb, tm):
    B, T, D = x.shape
    tok = pl.BlockSpec((None, tm, D), lambda b, i: (b, i, 0))
    consts = (wg, wu, wd, lg, lb)
    return pl.pallas_call(
        _ffn_kernel,
        grid=(B, T // tm),
        in_specs=[tok] + [_const_spec(c.shape) for c in consts],
        out_specs=tok,
        out_shape=jax.ShapeDtypeStruct((B, T, D), f32),
        compiler_params=_params(("parallel", "parallel")),
        name="ffn",
    )(x, *consts)


def _prepare_layer(l, w_in, gla_up_f, gla_up_b, gla_bias_f, gla_bias_b, gla_norm, pool_w, pool_scale,
                   na_rpb, w_br_a, w_br_b, w_br_c, w_out, ln1_g, ln1_b, w_gate, w_up, w_down,
                   ln2_g, ln2_b):
    wi = w_in[l]
    o_lr = 2 * GLA_KEY + 2 * GLA_VAL
    o_p = o_lr + 2 * GLA_RANK
    o_g = o_p + POOL_WIDTH + 3 * NA_WIDTH
    wlr = jnp.zeros((D_MODEL, LR_PAD), f32).at[:, :2 * GLA_RANK].set(wi[:, o_lr:o_p])
    uf = jnp.zeros((LR_PAD, GLA_KEY), f32).at[:GLA_RANK].set(gla_up_f[l])
    ub = jnp.zeros((LR_PAD, GLA_KEY), f32).at[GLA_RANK:2 * GLA_RANK].set(gla_up_b[l])
    row = lambda a: a.reshape(1, -1).astype(f32)
    o_og = 2 * GLA_KEY + GLA_VAL
    wa = jnp.concatenate([wi[:, :o_og], 0.5 * wi[:, o_og:o_lr]], axis=1)
    return dict(
        wa=wa.astype(bf16), wlr=wlr.astype(bf16), wb=wi[:, o_p:o_g].astype(bf16),
        wg=(0.5 * wi[:, o_g:]).astype(bf16),
        uf=uf.astype(bf16), ub=ub.astype(bf16), ubf=row(gla_bias_f[l]), ubb=row(gla_bias_b[l]),
        ng=row(gla_norm[l]),
        pw=jax.scipy.linalg.block_diag(*[pool_w[l, g] for g in range(POOL_GROUPS)]).astype(bf16),
        ps=row(pool_scale[l]), bias=_na_bias_table(na_rpb[l]),
        bra=(0.5 * w_br_a[l]).astype(bf16), brb=(0.5 * w_br_b[l]).astype(bf16),
        brc=(0.5 * w_br_c[l]).astype(bf16),
        wo=w_out[l].astype(bf16), ln1g=row(ln1_g[l]), ln1b=row(ln1_b[l]),
        wgate=w_gate[l].astype(bf16), wup=w_up[l].astype(bf16), wdown=w_down[l].astype(bf16),
        ln2g=row(ln2_g[l]), ln2b=row(ln2_b[l]))


def _trunk(x, layers, gla_cst, np_cst, tm):
    for p in layers:
        qk, v, og, lr, pin, qkv, gates = _inproj(x, p["wa"], p["wlr"], p["wb"], p["wg"], tm)
        on = _gla(qk, v, lr, p["uf"], p["ubf"], p["ub"], p["ubb"], p["ng"], gla_cst)
        yc, yb = _napool(qkv, pin, p["bias"], gla_cst["mkk"], p["pw"], p["ps"], np_cst)
        x = _merge(x, on, og, yb, yc, gates, p["bra"], p["brb"], p["brc"], p["wo"],
                   p["ln1g"], p["ln1b"], tm)
        x = _ffn(x, p["wgate"], p["wup"], p["wdown"], p["ln2g"], p["ln2b"], tm)
    return x


def kernel(x_prompt, x_sample, w_in, gla_up_f, gla_up_b, gla_bias_f, gla_bias_b, gla_norm, pool_w, pool_scale, na_rpb, w_br_a, w_br_b, w_br_c, w_out, ln1_g, ln1_b, w_gate, w_up, w_down, ln2_g, ln2_b):
    weights = (w_in, gla_up_f, gla_up_b, gla_bias_f, gla_bias_b, gla_norm, pool_w, pool_scale, na_rpb,
               w_br_a, w_br_b, w_br_c, w_out, ln1_g, ln1_b, w_gate, w_up, w_down, ln2_g, ln2_b)
    layers = [_prepare_layer(l, *weights) for l in range(w_in.shape[0])]
    gla_cst = _gla_constants()
    np_cst = _napool_constants()
    tm = 512
    return (_trunk(x_prompt, layers, gla_cst, np_cst, tm), _trunk(x_sample, layers, gla_cst, np_cst, tm))
```

```python
import functools

import numpy as np
import jax
import jax.numpy as jnp
from jax import lax
from jax.experimental import pallas as pl
from jax.experimental.pallas import tpu as pltpu

f32 = jnp.float32
bf16 = jnp.bfloat16

D_MODEL = 1024
DEPTH = 4
GRID_W = 64
GLA_HEADS = 4
GLA_DK = 64
GLA_DV = 128
GLA_KEY = GLA_HEADS * GLA_DK
GLA_VAL = GLA_HEADS * GLA_DV
GLA_RANK = 16
GLA_TAU = 16.0
GLA_CHUNK = 64
GLA_SUPER = 4
POOL_GROUPS = 4
POOL_GC = 64
POOL_WIDTH = POOL_GROUPS * POOL_GC
POOL_WINDOWS = (2, 4, 8, 16)
NA_HEADS = 4
NA_HD = 64
NA_WIDTH = NA_HEADS * NA_HD
NA_WIN_R = 8
NA_WIN_C = 16
N_BRANCH = 3
D_FF = 2816
DEEPNORM_ALPHA = (2 * DEPTH) ** 0.25
LN_EPS = 1e-5
RMS_EPS = 1e-6
LR_PAD = 128
NEG_BIG = -1e30

VMEM_LIMIT_BYTES = 56 * 1024 * 1024

_NN = (((1,), (0,)), ((), ()))
_NT = (((1,), (1,)), ((), ()))
_TN = (((0,), (0,)), ((), ()))


def _dot(a, b, dims=_NN):
    return lax.dot_general(a, b, dims, preferred_element_type=f32)


def _const_spec(shape):
    nd = len(shape)
    return pl.BlockSpec(shape, lambda *_: (0,) * nd, pipeline_mode=pl.Buffered(1))


def _params(sem):
    return pltpu.CompilerParams(dimension_semantics=sem, vmem_limit_bytes=VMEM_LIMIT_BYTES)


def _layer_norm(y, g, b):
    mu = jnp.mean(y, axis=-1, keepdims=True)
    yc = y - mu
    var = jnp.mean(yc * yc, axis=-1, keepdims=True)
    return yc * lax.rsqrt(var + LN_EPS) * g + b


def _silu(x):
    h = 0.5 * x
    return h * jnp.tanh(h) + h


def _gla_decays(lr, u_ref, ub_ref, tril_ref, triu_ref, n_chunks):
    L = GLA_CHUNK
    z = _dot(lr.astype(bf16), u_ref[...]) + ub_ref[...]
    la = (jnp.minimum(z, 0.0) - jnp.log(1.0 + jnp.exp(-jnp.abs(z)))) * (1.0 / GLA_TAU)
    hi = la.astype(bf16)
    lo = (la - hi.astype(f32)).astype(bf16)
    out = []
    for d, (tri_ref, last) in enumerate(((tril_ref, L - 1), (triu_ref, 0))):
        cols = slice(d * GLA_KEY, (d + 1) * GLA_KEY)
        tri = tri_ref[...]
        R = tri.shape[0]
        b = jnp.concatenate(
            [_dot(tri, hi[r:r + R, cols]) + _dot(tri, lo[r:r + R, cols]) for r in range(0, lr.shape[0], R)],
            axis=0)
        ends = b.reshape(n_chunks, L, GLA_KEY)[:, last, :]
        out.append((b, ends))
    return out


def _inproj_kernel(x_ref, wa_ref, wlr_ref, wb_ref, wg_ref, u_ref, ub_ref, tril_ref, triu_ref,
                   gq_ref, v_ref, og_ref, dec_ref, p_ref, qkv_ref, g_ref):
    tm = x_ref.shape[0]
    n_chunks = tm // GLA_CHUNK
    xb = x_ref[...].astype(bf16)
    ra = _dot(xb, wa_ref[...])
    v_ref[...] = ra[:, 2 * GLA_KEY:2 * GLA_KEY + GLA_VAL].astype(bf16)
    og_ref[...] = ra[:, 2 * GLA_KEY + GLA_VAL:].astype(bf16)
    lr = _dot(xb, wlr_ref[...])
    g_ref[...] = _dot(xb, wg_ref[...]).astype(bf16)
    q = ra[:, :GLA_KEY] * (GLA_DK ** -0.5)
    k = ra[:, GLA_KEY:2 * GLA_KEY]
    for d, (b, ends) in enumerate(_gla_decays(lr, u_ref, ub_ref, tril_ref, triu_ref, n_chunks)):
        o = 2 * d * GLA_KEY
        gq_ref[:, o:o + GLA_KEY] = (q * jnp.exp(b)).astype(bf16)
        gq_ref[:, o + GLA_KEY:o + 2 * GLA_KEY] = (k * jnp.exp(-b)).astype(bf16)
        dec_ref[:, d * GLA_KEY:(d + 1) * GLA_KEY] = jnp.exp(ends)
    rb = _dot(xb, wb_ref[...])
    p_ref[...] = rb[:, :POOL_WIDTH].astype(bf16)
    qkv_ref[...] = rb[:, POOL_WIDTH:].astype(bf16)


def _inproj(x, wa, wlr, wb, wg, u, ub, cst, tm):
    B, T, D = x.shape
    n_chunks = tm // GLA_CHUNK
    widths = (4 * GLA_KEY, GLA_VAL, GLA_VAL, None, POOL_WIDTH, 3 * NA_WIDTH, N_BRANCH * D_MODEL)
    tok = lambda w: pl.BlockSpec((None, tm, w), lambda b, i: (b, i, 0))
    dec_spec = pl.BlockSpec((None, n_chunks, 2 * GLA_KEY), lambda b, i: (b, i, 0))
    consts = (wa, wlr, wb, wg, u, ub, cst["tril"], cst["triu"])
    return pl.pallas_call(
        _inproj_kernel,
        grid=(B, T // tm),
        in_specs=[tok(D)] + [_const_spec(c.shape) for c in consts],
        out_specs=[dec_spec if w is None else tok(w) for w in widths],
        out_shape=[jax.ShapeDtypeStruct((B, T // GLA_CHUNK, 2 * GLA_KEY), f32) if w is None
                   else jax.ShapeDtypeStruct((B, T, w), bf16) for w in widths],
        compiler_params=_params(("parallel", "parallel")),
        name="inproj",
    )(x, *consts)


def _inproj_constants():
    L, n = GLA_CHUNK, GLA_SUPER
    i = np.arange(L)
    tril = (i[None, :] <= i[:, None]).astype(np.float32)
    eye = np.eye(n, dtype=np.float32)
    return dict(tril=jnp.asarray(np.kron(eye, tril), bf16), triu=jnp.asarray(np.kron(eye, tril.T), bf16))


def _gla_super(sc, gq_ref, v_ref, dec_ref, cm_ref, mkk_ref, mkv_ref, hm_ref, st_ref, fwd):
    L, NS, H = GLA_CHUNK, GLA_SUPER, GLA_HEADS
    R = L * NS
    r0 = pl.multiple_of(sc * R, R)
    o = 0 if fwd else 2 * GLA_KEY
    q_dec = gq_ref[pl.ds(r0, R), o:o + GLA_KEY]
    k_inv = gq_ref[pl.ds(r0, R), o + GLA_KEY:o + 2 * GLA_KEY]
    v = v_ref[pl.ds(r0, R), :]
    dcol = 0 if fwd else GLA_KEY
    hm = hm_ref[...]
    state = [st_ref[p] for p in range(H // 2)]
    outs = [None] * NS
    for c in (range(NS) if fwd else range(NS - 1, -1, -1)):
        rows = slice(c * L, (c + 1) * L)
        qc, kic, vc = q_dec[rows], k_inv[rows], v[rows]
        dec = dec_ref[pl.ds(sc * NS + c, 1), dcol:dcol + GLA_KEY]
        kec = (kic.astype(f32) * dec).astype(bf16)
        kb = jnp.concatenate([kic] * H, axis=0) * mkk_ref[...]
        att = _dot(qc, kb, _NT) * cm_ref[...]
        vb = jnp.concatenate([vc] * H, axis=0) * mkv_ref[...]
        o_c = _dot(att.astype(bf16), vb)
        inter = []
        for p in range(H // 2):
            lanes = slice(p * 2 * GLA_DK, (p + 1) * 2 * GLA_DK)
            q2 = jnp.concatenate([qc[:, lanes] * hm[0:1], qc[:, lanes] * hm[1:2]], axis=0)
            oi = _dot(q2, state[p].astype(bf16), _NT)
            inter += [oi[0:L], oi[L:2 * L]]
            ke2 = jnp.concatenate([kec[:, lanes] * hm[0:1], kec[:, lanes] * hm[1:2]], axis=0)
            v2 = jnp.concatenate([vc[:, (2 * p) * GLA_DV:(2 * p + 1) * GLA_DV],
                                  vc[:, (2 * p + 1) * GLA_DV:(2 * p + 2) * GLA_DV]], axis=0)
            state[p] = dec[:, lanes] * state[p] + _dot(v2, ke2, _TN)
        outs[c] = o_c + jnp.concatenate(inter, axis=1)
    for p in range(H // 2):
        st_ref[p] = state[p]
    return jnp.concatenate(outs, axis=0)


def _gla_kernel(gq_ref, v_ref, dec_ref, ng_ref, cml_ref, cmu_ref, mkk_ref, mkv_ref, hm_ref,
                o_ref, acc_ref, sf_ref, sb_ref, *, n_super):
    R = GLA_CHUNK * GLA_SUPER
    sf_ref[...] = jnp.zeros_like(sf_ref)
    sb_ref[...] = jnp.zeros_like(sb_ref)
    shared = (mkk_ref, mkv_ref, hm_ref)

    def both(i):
        cf, cb = i, n_super - 1 - i
        of = _gla_super(cf, gq_ref, v_ref, dec_ref, cml_ref, *shared, sf_ref, True)
        ob = _gla_super(cb, gq_ref, v_ref, dec_ref, cmu_ref, *shared, sb_ref, False)
        return cf, cb, of, ob

    def first_half(i, carry):
        cf, cb, of, ob = both(i)
        acc_ref[pl.ds(pl.multiple_of(cf * R, R), R), :] = of
        acc_ref[pl.ds(pl.multiple_of(cb * R, R), R), :] = ob
        return carry

    def finish(c, o_new):
        r0 = pl.multiple_of(c * R, R)
        o = acc_ref[pl.ds(r0, R), :] + o_new
        outs = []
        for h in range(GLA_HEADS):
            oh = o[:, h * GLA_DV:(h + 1) * GLA_DV]
            ms = jnp.mean(oh * oh, axis=-1, keepdims=True)
            outs.append(oh * lax.rsqrt(ms + RMS_EPS) * ng_ref[...])
        o_ref[pl.ds(r0, R), :] = jnp.concatenate(outs, axis=1).astype(o_ref.dtype)

    def second_half(i, carry):
        cf, cb, of, ob = both(i)
        finish(cf, of)
        finish(cb, ob)
        return carry

    lax.fori_loop(0, n_super // 2, first_half, 0)
    lax.fori_loop(n_super // 2, n_super, second_half, 0)


def _gla_constants():
    L, H = GLA_CHUNK, GLA_HEADS
    i = np.arange(L)
    tril = (i[None, :] <= i[:, None]).astype(np.float32)
    hk = np.arange(H * L) // L
    hv = np.arange(H * GLA_DV) // GLA_DV
    mkk = (hk[:, None] == hk[None, :]).astype(np.float32)
    mkv = (hk[:, None] == hv[None, :]).astype(np.float32)
    half = (np.arange(2)[:, None] == (np.arange(2 * GLA_DK) // GLA_DK)[None, :]).astype(np.float32)
    return dict(
        cml=jnp.asarray(np.tile(tril, (1, H)), f32), cmu=jnp.asarray(np.tile(tril.T, (1, H)), f32),
        mkk=jnp.asarray(mkk, bf16), mkv=jnp.asarray(mkv, bf16), hm=jnp.asarray(half, bf16))


def _gla(gq, v, dec, ng, cst):
    B, T, _ = gq.shape
    R = GLA_CHUNK * GLA_SUPER
    assert T % (2 * R) == 0
    seq = lambda a: pl.BlockSpec((None,) + a.shape[1:], lambda b: (b, 0, 0))
    consts = (ng, cst["cml"], cst["cmu"], cst["mkk"], cst["mkv"], cst["hm"])
    state = pltpu.VMEM((GLA_HEADS // 2, GLA_DV, 2 * GLA_DK), f32)
    return pl.pallas_call(
        functools.partial(_gla_kernel, n_super=T // R),
        grid=(B,),
        in_specs=[seq(gq), seq(v), seq(dec)] + [_const_spec(c.shape) for c in consts],
        out_specs=pl.BlockSpec((None, T, GLA_VAL), lambda b: (b, 0, 0)),
        out_shape=jax.ShapeDtypeStruct((B, T, GLA_VAL), bf16),
        scratch_shapes=[pltpu.VMEM((T, GLA_VAL), f32), state, state],
        compiler_params=_params(("parallel",)),
        name="gla",
    )(gq, v, dec, *consts)


def _napool_kernel(qkv_ref, p_ref, bias_ref, mkk_ref, hm_ref, a_ref, half_ref, pw_ref, ps_ref,
                   yc_ref, yb_ref, s0_ref, s1_ref, e0_ref, e1_ref, d0_ref, d1_ref, *, rows, seq_len):
    W = GRID_W
    wr = NA_WIN_R

    def window(r):
        r0 = jnp.clip(r - wr // 2, 0, rows - wr)
        return r0, pl.multiple_of(r0 * W, W)

    def scores(r, s_ref):
        r = jnp.minimum(r, rows - 1)
        t0 = pl.multiple_of(r * W, W)
        r0, k0 = window(r)
        q = qkv_ref[pl.ds(t0, W), 0:NA_WIDTH] * (NA_HD ** -0.5)
        kw = qkv_ref[pl.ds(k0, wr * W), NA_WIDTH:2 * NA_WIDTH]
        qs = jnp.concatenate([q] * NA_HEADS, axis=0) * mkk_ref[...]
        s_ref[...] = _dot(qs, kw, _NT) + bias_ref[r - r0]

    def softmax(s_ref, e_ref, d_ref):
        s = s_ref[...]
        e = jnp.exp(s - jnp.max(s, axis=-1, keepdims=True))
        d_ref[...] = jnp.sum(e, axis=-1, keepdims=True)
        e_ref[...] = e.astype(bf16)

    def finish(r, e_ref, d_ref):
        t0 = pl.multiple_of(r * W, W)
        _, k0 = window(r)
        vw = qkv_ref[pl.ds(k0, wr * W), 2 * NA_WIDTH:3 * NA_WIDTH]
        pv = _dot(e_ref[...], vw) / d_ref[...]
        hm = hm_ref[...]
        out = pv[0:W] * hm[0:1]
        for h in range(1, NA_HEADS):
            out = out + pv[h * W:(h + 1) * W] * hm[h:h + 1]
        yc_ref[pl.ds(t0, W), :] = out.astype(yc_ref.dtype)

        pc = p_ref[pl.ds(t0, W), :]
        tp = pl.multiple_of(jnp.maximum(r - 1, 0) * W, W)
        tn = pl.multiple_of(jnp.minimum(r + 1, rows - 1) * W, W)
        pp = p_ref[pl.ds(tp, W), :]
        pn = p_ref[pl.ds(tn, W), :]
        pp = jnp.where(r > 0, pp, jnp.zeros_like(pp))
        pn = jnp.where(r < rows - 1, pn, jnp.zeros_like(pn))
        p3 = jnp.concatenate([pp, pc, pn], axis=0)
        wall = _dot(a_ref[...], p3)
        wsum = wall[0:W] * hm[0:1]
        for g in range(1, POOL_GROUPS):
            wsum = wsum + wall[g * W:(g + 1) * W] * hm[g:g + 1]
        t = t0 + lax.broadcasted_iota(jnp.int32, (W, POOL_WIDTH), 0)
        half = half_ref[...]
        cnt = (jnp.minimum(t + half, seq_len) - jnp.maximum(t - half, 0)).astype(f32)
        d = (wsum / cnt - pc.astype(f32)).astype(bf16)
        yb_ref[pl.ds(t0, W), :] = (_dot(d, pw_ref[...]) * ps_ref[...]).astype(yb_ref.dtype)

    scores(0, s0_ref)
    softmax(s0_ref, e0_ref, d0_ref)
    scores(1, s1_ref)

    def body(j, carry):
        r = 2 * j
        finish(r, e0_ref, d0_ref)
        softmax(s1_ref, e1_ref, d1_ref)
        scores(r + 2, s0_ref)
        finish(r + 1, e1_ref, d1_ref)
        softmax(s0_ref, e0_ref, d0_ref)
        scores(r + 3, s1_ref)
        return carry

    lax.fori_loop(0, rows // 2, body, 0)


def _napool_constants():
    W = GRID_W
    hk = np.arange(NA_WIDTH) // NA_HD
    hm = (np.arange(NA_HEADS)[:, None] == hk[None, :]).astype(np.float32)
    a = np.zeros((POOL_GROUPS, W, 3 * W), np.float32)
    t = np.arange(W)[:, None]
    j = np.arange(3 * W)[None, :] - W
    for g, w in enumerate(POOL_WINDOWS):
        a[g] = ((j >= t - w // 2) & (j < t + w - w // 2)).astype(np.float32)
    half = np.repeat(np.array([w // 2 for w in POOL_WINDOWS], np.int32), POOL_GC)[None, :]
    a = a.reshape(POOL_GROUPS * W, 3 * W)
    return dict(hm=jnp.asarray(hm, f32), a=jnp.asarray(a, bf16), half=jnp.asarray(half, jnp.int32))


def _na_bias_table(rpb):
    W, R, C = GRID_W, NA_WIN_R, NA_WIN_C
    cols = np.arange(W)
    c0 = np.clip(cols - C // 2, 0, W - C)
    delta = np.arange(R)[:, None]
    kr = np.arange(R)[None, :]
    row_off = kr - delta + (R - 1)
    col_off = cols[None, :] - cols[:, None] + (C - 1)
    valid = (cols[None, :] >= c0[:, None]) & (cols[None, :] < c0[:, None] + C)
    col_off = np.clip(col_off, 0, 2 * C - 2)
    tab = rpb[:, row_off][:, :, :, col_off]
    tab = jnp.where(jnp.asarray(valid)[None, None, None], tab, NEG_BIG)
    tab = jnp.transpose(tab, (1, 0, 3, 2, 4))
    return tab.reshape(R, NA_HEADS * W, R * W).astype(f32)


def _napool(qkv, p, bias, mkk, pw, ps, cst):
    B, T, _ = qkv.shape
    rows = T // GRID_W
    assert rows >= NA_WIN_R and rows % 2 == 0
    seq = lambda w: pl.BlockSpec((None, T, w), lambda b: (b, 0, 0))
    consts = (bias, mkk, cst["hm"], cst["a"], cst["half"], pw, ps)
    return pl.pallas_call(
        functools.partial(_napool_kernel, rows=rows, seq_len=T),
        grid=(B,),
        in_specs=[seq(3 * NA_WIDTH), seq(POOL_WIDTH)] + [_const_spec(c.shape) for c in consts],
        out_specs=[seq(NA_WIDTH), seq(POOL_WIDTH)],
        out_shape=[jax.ShapeDtypeStruct((B, T, NA_WIDTH), bf16),
                   jax.ShapeDtypeStruct((B, T, POOL_WIDTH), bf16)],
        scratch_shapes=[pltpu.VMEM((NA_HEADS * GRID_W, NA_WIN_R * GRID_W), f32)] * 2
        + [pltpu.VMEM((NA_HEADS * GRID_W, NA_WIN_R * GRID_W), bf16)] * 2
        + [pltpu.VMEM((NA_HEADS * GRID_W, 1), f32)] * 2,
        compiler_params=_params(("parallel",)),
        name="napool",
    )(qkv, p, *consts)


def _merge_kernel(x_ref, on_ref, og_ref, yb_ref, yc_ref, g_ref, wa_ref, wb_ref, wc_ref, wo_ref,
                  lg_ref, lb_ref, o_ref):
    D = D_MODEL
    one = jnp.ones((), bf16)
    tm = x_ref.shape[0]
    for rows in (pl.ds(0, tm // 2), pl.ds(tm // 2, tm // 2)):
        h = og_ref[rows, :]
        a_in = on_ref[rows, :] * (h * jnp.tanh(h) + h)
        m = ((one + jnp.tanh(g_ref[rows, 0:D])) * _dot(a_in, wa_ref[...]).astype(bf16)
             + (one + jnp.tanh(g_ref[rows, D:2 * D])) * _dot(yb_ref[rows, :], wb_ref[...]).astype(bf16)
             + (one + jnp.tanh(g_ref[rows, 2 * D:3 * D])) * _dot(yc_ref[rows, :], wc_ref[...]).astype(bf16))
        y = DEEPNORM_ALPHA * x_ref[rows, :] + _dot(m, wo_ref[...])
        o_ref[rows, :] = _layer_norm(y, lg_ref[...], lb_ref[...])


def _merge(x, on, og, yb, yc, g, wa, wb, wc, wo, lg, lb, tm):
    B, T, D = x.shape
    tok = lambda w: pl.BlockSpec((None, tm, w), lambda b, i: (b, i, 0))
    consts = (wa, wb, wc, wo, lg, lb)
    return pl.pallas_call(
        _merge_kernel,
        grid=(B, T // tm),
        in_specs=[tok(D), tok(GLA_VAL), tok(GLA_VAL), tok(POOL_WIDTH), tok(NA_WIDTH),
                  tok(N_BRANCH * D)] + [_const_spec(c.shape) for c in consts],
        out_specs=tok(D),
        out_shape=jax.ShapeDtypeStruct((B, T, D), f32),
        compiler_params=_params(("parallel", "parallel")),
        name="merge",
    )(x, on, og, yb, yc, g, *consts)


def _ffn_kernel(x_ref, wg_ref, wu_ref, wd_ref, lg_ref, lb_ref, o_ref):
    tm = x_ref.shape[0]
    for rows in (pl.ds(0, tm // 2), pl.ds(tm // 2, tm // 2)):
        x = x_ref[rows, :]
        xb = x.astype(bf16)
        hg = _dot(xb, wg_ref[...])
        hu = _dot(xb, wu_ref[...])
        act = (_silu(hg) * hu).astype(bf16)
        y = DEEPNORM_ALPHA * x + _dot(act, wd_ref[...])
        o_ref[rows, :] = _layer_norm(y, lg_ref[...], lb_ref[...])


def _ffn(x, wg, wu, wd, lg, lb, tm):
    B, T, D = x.shape
    tok = pl.BlockSpec((None, tm, D), lambda b, i: (b, i, 0))
    consts = (wg, wu, wd, lg, lb)
    return pl.pallas_call(
        _ffn_kernel,
        grid=(B, T // tm),
        in_specs=[tok] + [_const_spec(c.shape) for c in consts],
        out_specs=tok,
        out_shape=jax.ShapeDtypeStruct((B, T, D), f32),
        compiler_params=_params(("parallel", "parallel")),
        name="ffn",
    )(x, *consts)


def _prepare_layer(l, w_in, gla_up_f, gla_up_b, gla_bias_f, gla_bias_b, gla_norm, pool_w, pool_scale,
                   na_rpb, w_br_a, w_br_b, w_br_c, w_out, ln1_g, ln1_b, w_gate, w_up, w_down,
                   ln2_g, ln2_b):
    wi = w_in[l]
    o_lr = 2 * GLA_KEY + 2 * GLA_VAL
    o_p = o_lr + 2 * GLA_RANK
    o_g = o_p + POOL_WIDTH + 3 * NA_WIDTH
    wlr = jnp.zeros((D_MODEL, LR_PAD), f32).at[:, :2 * GLA_RANK].set(wi[:, o_lr:o_p])
    u = jnp.zeros((LR_PAD, 2 * GLA_KEY), f32)
    u = u.at[:GLA_RANK, :GLA_KEY].set(gla_up_f[l]).at[GLA_RANK:2 * GLA_RANK, GLA_KEY:].set(gla_up_b[l])
    row = lambda a: a.reshape(1, -1).astype(f32)
    o_og = 2 * GLA_KEY + GLA_VAL
    wa = jnp.concatenate([wi[:, :o_og], 0.5 * wi[:, o_og:o_lr]], axis=1)
    return dict(
        wa=wa.astype(bf16), wlr=wlr.astype(bf16), wb=wi[:, o_p:o_g].astype(bf16),
        wg=(0.5 * wi[:, o_g:]).astype(bf16),
        u=u.astype(bf16), ub=row(jnp.concatenate([gla_bias_f[l], gla_bias_b[l]])),
        ng=row(gla_norm[l]),
        pw=jax.scipy.linalg.block_diag(*[pool_w[l, g] for g in range(POOL_GROUPS)]).astype(bf16),
        ps=row(pool_scale[l]), bias=_na_bias_table(na_rpb[l]),
        bra=(0.5 * w_br_a[l]).astype(bf16), brb=(0.5 * w_br_b[l]).astype(bf16),
        brc=(0.5 * w_br_c[l]).astype(bf16),
        wo=w_out[l].astype(bf16), ln1g=row(ln1_g[l]), ln1b=row(ln1_b[l]),
        wgate=w_gate[l].astype(bf16), wup=w_up[l].astype(bf16), wdown=w_down[l].astype(bf16),
        ln2g=row(ln2_g[l]), ln2b=row(ln2_b[l]))


def _trunk(x, layers, in_cst, gla_cst, np_cst, tm):
    for p in layers:
        gq, v, og, dec, pin, qkv, gates = _inproj(x, p["wa"], p["wlr"], p["wb"], p["wg"], p["u"], p["ub"],
                                                  in_cst, tm)
        on = _gla(gq, v, dec, p["ng"], gla_cst)
        yc, yb = _napool(qkv, pin, p["bias"], gla_cst["mkk"], p["pw"], p["ps"], np_cst)
        x = _merge(x, on, og, yb, yc, gates, p["bra"], p["brb"], p["brc"], p["wo"],
                   p["ln1g"], p["ln1b"], tm)
        x = _ffn(x, p["wgate"], p["wup"], p["wdown"], p["ln2g"], p["ln2b"], tm)
    return x


def kernel(x_prompt, x_sample, w_in, gla_up_f, gla_up_b, gla_bias_f, gla_bias_b, gla_norm, pool_w, pool_scale, na_rpb, w_br_a, w_br_b, w_br_c, w_out, ln1_g, ln1_b, w_gate, w_up, w_down, ln2_g, ln2_b):
    weights = (w_in, gla_up_f, gla_up_b, gla_bias_f, gla_bias_b, gla_norm, pool_w, pool_scale, na_rpb,
               w_br_a, w_br_b, w_br_c, w_out, ln1_g, ln1_b, w_gate, w_up, w_down, ln2_g, ln2_b)
    layers = [_prepare_layer(l, *weights) for l in range(w_in.shape[0])]
    tm = 512
    in_cst = _inproj_constants()
    gla_cst = _gla_constants()
    np_cst = _napool_constants()
    return (_trunk(x_prompt, layers, in_cst, gla_cst, np_cst, tm),
            _trunk(x_sample, layers, in_cst, gla_cst, np_cst, tm))
```

```python
import functools

import numpy as np
import jax
import jax.numpy as jnp
from jax import lax
from jax.experimental import pallas as pl
from jax.experimental.pallas import tpu as pltpu

f32 = jnp.float32
bf16 = jnp.bfloat16

D_MODEL = 1024
DEPTH = 4
GRID_W = 64
GLA_HEADS = 4
GLA_DK = 64
GLA_DV = 128
GLA_KEY = GLA_HEADS * GLA_DK
GLA_VAL = GLA_HEADS * GLA_DV
GLA_RANK = 16
GLA_TAU = 16.0
GLA_CHUNK = 64
GLA_SUPER = 4
POOL_GROUPS = 4
POOL_GC = 64
POOL_WIDTH = POOL_GROUPS * POOL_GC
POOL_WINDOWS = (2, 4, 8, 16)
NA_HEADS = 4
NA_HD = 64
NA_WIDTH = NA_HEADS * NA_HD
NA_WIN_R = 8
NA_WIN_C = 16
N_BRANCH = 3
D_FF = 2816
DEEPNORM_ALPHA = (2 * DEPTH) ** 0.25
LN_EPS = 1e-5
RMS_EPS = 1e-6
LR_PAD = 128
NEG_BIG = -1e30
LOG2E = 1.4426950408889634

DENSE_TILE = 1024
ROW_PIECE = 256
INPROJ_TILE = 512

VMEM_LIMIT_BYTES = 56 * 1024 * 1024

_NN = (((1,), (0,)), ((), ()))
_NT = (((1,), (1,)), ((), ()))
_TN = (((0,), (0,)), ((), ()))


def _dot(a, b, dims=_NN):
    return lax.dot_general(a, b, dims, preferred_element_type=f32)


def _const_spec(shape):
    nd = len(shape)
    return pl.BlockSpec(shape, lambda *_: (0,) * nd, pipeline_mode=pl.Buffered(1))


def _params(sem):
    return pltpu.CompilerParams(dimension_semantics=sem, vmem_limit_bytes=VMEM_LIMIT_BYTES)


def _layer_norm(y, g, b):
    mu = jnp.mean(y, axis=-1, keepdims=True)
    yc = y - mu
    var = jnp.mean(yc * yc, axis=-1, keepdims=True)
    return yc * lax.rsqrt(var + LN_EPS) * g + b


def _row_pieces(tm):
    return [pl.ds(r, ROW_PIECE) for r in range(0, tm, ROW_PIECE)]


def _silu(x):
    h = 0.5 * x
    return h * jnp.tanh(h) + h


def _gla_decays(lr, u_ref, ub_ref, tril_ref, triu_ref):
    L = GLA_CHUNK
    tm = lr.shape[0]
    z = _dot(lr.astype(bf16), u_ref[...]) + ub_ref[...]
    la = (jnp.minimum(z, 0.0) - jnp.log(1.0 + jnp.exp(-jnp.abs(z)))) * (1.0 / GLA_TAU)
    hi = la.astype(bf16)
    lo = (la - hi.astype(f32)).astype(bf16)
    out = []
    for d, (tri_ref, last) in enumerate(((tril_ref, L - 1), (triu_ref, 0))):
        cols = slice(d * GLA_KEY, (d + 1) * GLA_KEY)
        tri = tri_ref[...]
        R = tri.shape[0]
        b = jnp.concatenate(
            [_dot(tri, hi[r:r + R, cols]) + _dot(tri, lo[r:r + R, cols]) for r in range(0, tm, R)],
            axis=0)
        ends = b.reshape(tm // L, L, GLA_KEY)[:, last, :]
        out.append((b, ends))
    return out


def _inproj_kernel(x_ref, wa_ref, wlr_ref, wb_ref, wg_ref, u_ref, ub_ref, tril_ref, triu_ref,
                   gq_ref, v_ref, og_ref, dec_ref, p_ref, qkv_ref, g_ref):
    xb = x_ref[...].astype(bf16)
    ra = _dot(xb, wa_ref[...])
    v_ref[...] = ra[:, 2 * GLA_KEY:2 * GLA_KEY + GLA_VAL].astype(bf16)
    og_ref[...] = ra[:, 2 * GLA_KEY + GLA_VAL:].astype(bf16)
    lr = _dot(xb, wlr_ref[...])
    g_ref[...] = _dot(xb, wg_ref[...]).astype(bf16)
    q = ra[:, :GLA_KEY] * (GLA_DK ** -0.5)
    k = ra[:, GLA_KEY:2 * GLA_KEY]
    for d, (b, ends) in enumerate(_gla_decays(lr, u_ref, ub_ref, tril_ref, triu_ref)):
        o = 2 * d * GLA_KEY
        gq_ref[:, o:o + GLA_KEY] = (q * jnp.exp(b)).astype(bf16)
        gq_ref[:, o + GLA_KEY:o + 2 * GLA_KEY] = (k * jnp.exp(-b)).astype(bf16)
        dec_ref[:, d * GLA_KEY:(d + 1) * GLA_KEY] = jnp.exp(ends)
    rb = _dot(xb, wb_ref[...])
    p_ref[...] = rb[:, :POOL_WIDTH].astype(bf16)
    qkv_ref[...] = rb[:, POOL_WIDTH:].astype(bf16)


def _inproj(x, wa, wlr, wb, wg, u, ub, cst, tm):
    B, T, D = x.shape
    assert tm % (GLA_CHUNK * GLA_SUPER) == 0
    n_chunks = tm // GLA_CHUNK
    widths = (4 * GLA_KEY, GLA_VAL, GLA_VAL, None, POOL_WIDTH, 3 * NA_WIDTH, N_BRANCH * D_MODEL)
    tok = lambda w: pl.BlockSpec((None, tm, w), lambda b, i: (b, i, 0))
    dec_spec = pl.BlockSpec((None, n_chunks, 2 * GLA_KEY), lambda b, i: (b, i, 0))
    consts = (wa, wlr, wb, wg, u, ub, cst["tril"], cst["triu"])
    return pl.pallas_call(
        _inproj_kernel,
        grid=(B, T // tm),
        in_specs=[tok(D)] + [_const_spec(c.shape) for c in consts],
        out_specs=[dec_spec if w is None else tok(w) for w in widths],
        out_shape=[jax.ShapeDtypeStruct((B, T // GLA_CHUNK, 2 * GLA_KEY), f32) if w is None
                   else jax.ShapeDtypeStruct((B, T, w), bf16) for w in widths],
        compiler_params=_params(("parallel", "parallel")),
        name="inproj",
    )(x, *consts)


def _inproj_constants():
    L, n = GLA_CHUNK, GLA_SUPER
    i = np.arange(L)
    tril = (i[None, :] <= i[:, None]).astype(np.float32)
    eye = np.eye(n, dtype=np.float32)
    return dict(tril=jnp.asarray(np.kron(eye, tril), bf16), triu=jnp.asarray(np.kron(eye, tril.T), bf16))


def _gla_super(sc, gq_ref, v_ref, dec_ref, cm_ref, mkk_ref, mkv_ref, hm_ref, st_ref, fwd):
    L, NS, H = GLA_CHUNK, GLA_SUPER, GLA_HEADS
    R = L * NS
    r0 = pl.multiple_of(sc * R, R)
    o = 0 if fwd else 2 * GLA_KEY
    q_dec = gq_ref[pl.ds(r0, R), o:o + GLA_KEY]
    k_inv = gq_ref[pl.ds(r0, R), o + GLA_KEY:o + 2 * GLA_KEY]
    v = v_ref[pl.ds(r0, R), :]
    dcol = 0 if fwd else GLA_KEY
    hm = hm_ref[...]
    state = [st_ref[p] for p in range(H // 2)]
    outs = [None] * NS
    for c in (range(NS) if fwd else range(NS - 1, -1, -1)):
        rows = slice(c * L, (c + 1) * L)
        qc, kic, vc = q_dec[rows], k_inv[rows], v[rows]
        dec = dec_ref[pl.ds(sc * NS + c, 1), dcol:dcol + GLA_KEY]
        kec = (kic.astype(f32) * dec).astype(bf16)
        kb = jnp.concatenate([kic] * H, axis=0) * mkk_ref[...]
        att = _dot(qc, kb, _NT) * cm_ref[...]
        vb = jnp.concatenate([vc] * H, axis=0) * mkv_ref[...]
        o_c = _dot(att.astype(bf16), vb)
        inter = []
        for p in range(H // 2):
            lanes = slice(p * 2 * GLA_DK, (p + 1) * 2 * GLA_DK)
            q2 = jnp.concatenate([qc[:, lanes] * hm[0:1], qc[:, lanes] * hm[1:2]], axis=0)
            oi = _dot(q2, state[p].astype(bf16), _NT)
            inter += [oi[0:L], oi[L:2 * L]]
            ke2 = jnp.concatenate([kec[:, lanes] * hm[0:1], kec[:, lanes] * hm[1:2]], axis=0)
            v2 = jnp.concatenate([vc[:, (2 * p) * GLA_DV:(2 * p + 1) * GLA_DV],
                                  vc[:, (2 * p + 1) * GLA_DV:(2 * p + 2) * GLA_DV]], axis=0)
            state[p] = dec[:, lanes] * state[p] + _dot(v2, ke2, _TN)
        outs[c] = o_c + jnp.concatenate(inter, axis=1)
    for p in range(H // 2):
        st_ref[p] = state[p]
    return jnp.concatenate(outs, axis=0)


def _gla_kernel(gq_ref, v_ref, dec_ref, ng_ref, cml_ref, cmu_ref, mkk_ref, mkv_ref, hm_ref,
                o_ref, acc_ref, sf_ref, sb_ref, *, n_super):
    R = GLA_CHUNK * GLA_SUPER
    sf_ref[...] = jnp.zeros_like(sf_ref)
    sb_ref[...] = jnp.zeros_like(sb_ref)
    shared = (mkk_ref, mkv_ref, hm_ref)

    def both(i):
        cf, cb = i, n_super - 1 - i
        of = _gla_super(cf, gq_ref, v_ref, dec_ref, cml_ref, *shared, sf_ref, True)
        ob = _gla_super(cb, gq_ref, v_ref, dec_ref, cmu_ref, *shared, sb_ref, False)
        return cf, cb, of, ob

    def first_half(i, carry):
        cf, cb, of, ob = both(i)
        acc_ref[pl.ds(pl.multiple_of(cf * R, R), R), :] = of
        acc_ref[pl.ds(pl.multiple_of(cb * R, R), R), :] = ob
        return carry

    def finish(c, o_new):
        r0 = pl.multiple_of(c * R, R)
        o = acc_ref[pl.ds(r0, R), :] + o_new
        outs = []
        for h in range(GLA_HEADS):
            oh = o[:, h * GLA_DV:(h + 1) * GLA_DV]
            ms = jnp.mean(oh * oh, axis=-1, keepdims=True)
            outs.append(oh * lax.rsqrt(ms + RMS_EPS) * ng_ref[...])
        o_ref[pl.ds(r0, R), :] = jnp.concatenate(outs, axis=1).astype(o_ref.dtype)

    def second_half(i, carry):
        cf, cb, of, ob = both(i)
        finish(cf, of)
        finish(cb, ob)
        return carry

    lax.fori_loop(0, n_super // 2, first_half, 0)
    lax.fori_loop(n_super // 2, n_super, second_half, 0)


def _gla_constants():
    L, H = GLA_CHUNK, GLA_HEADS
    i = np.arange(L)
    tril = (i[None, :] <= i[:, None]).astype(np.float32)
    hk = np.arange(H * L) // L
    hv = np.arange(H * GLA_DV) // GLA_DV
    mkk = (hk[:, None] == hk[None, :]).astype(np.float32)
    mkv = (hk[:, None] == hv[None, :]).astype(np.float32)
    half = (np.arange(2)[:, None] == (np.arange(2 * GLA_DK) // GLA_DK)[None, :]).astype(np.float32)
    return dict(
        cml=jnp.asarray(np.tile(tril, (1, H)), f32), cmu=jnp.asarray(np.tile(tril.T, (1, H)), f32),
        mkk=jnp.asarray(mkk, bf16), mkv=jnp.asarray(mkv, bf16), hm=jnp.asarray(half, bf16))


def _gla(gq, v, dec, ng, cst):
    B, T, _ = gq.shape
    R = GLA_CHUNK * GLA_SUPER
    assert T % (2 * R) == 0
    seq = lambda a: pl.BlockSpec((None,) + a.shape[1:], lambda b: (b, 0, 0))
    consts = (ng, cst["cml"], cst["cmu"], cst["mkk"], cst["mkv"], cst["hm"])
    state = pltpu.VMEM((GLA_HEADS // 2, GLA_DV, 2 * GLA_DK), f32)
    return pl.pallas_call(
        functools.partial(_gla_kernel, n_super=T // R),
        grid=(B,),
        in_specs=[seq(gq), seq(v), seq(dec)] + [_const_spec(c.shape) for c in consts],
        out_specs=pl.BlockSpec((None, T, GLA_VAL), lambda b: (b, 0, 0)),
        out_shape=jax.ShapeDtypeStruct((B, T, GLA_VAL), bf16),
        scratch_shapes=[pltpu.VMEM((T, GLA_VAL), f32), state, state],
        compiler_params=_params(("parallel",)),
        name="gla",
    )(gq, v, dec, *consts)


def _napool_kernel(qkv_ref, p_ref, bias_ref, mkk_ref, hm_ref, a_ref, half_ref, pw_ref, ps_ref,
                   yc_ref, yb_ref, s0_ref, s1_ref, e0_ref, e1_ref, d0_ref, d1_ref, *, rows, seq_len):
    W = GRID_W
    wr = NA_WIN_R

    def window(r):
        r0 = jnp.clip(r - wr // 2, 0, rows - wr)
        return r0, pl.multiple_of(r0 * W, W)

    def scores(r, s_ref):
        r = jnp.minimum(r, rows - 1)
        t0 = pl.multiple_of(r * W, W)
        r0, k0 = window(r)
        q = (qkv_ref[pl.ds(t0, W), 0:NA_WIDTH].astype(f32) * (NA_HD ** -0.5 * LOG2E)).astype(bf16)
        kw = qkv_ref[pl.ds(k0, wr * W), NA_WIDTH:2 * NA_WIDTH]
        qs = jnp.concatenate([q] * NA_HEADS, axis=0) * mkk_ref[...]
        s_ref[...] = _dot(qs, kw, _NT) + bias_ref[r - r0]

    def softmax(s_ref, e_ref, d_ref):
        s = s_ref[...]
        e = jnp.exp2(s - jnp.max(s, axis=-1, keepdims=True))
        d_ref[...] = 1.0 / jnp.sum(e, axis=-1, keepdims=True)
        e_ref[...] = e.astype(bf16)

    def finish(r, e_ref, d_ref):
        t0 = pl.multiple_of(r * W, W)
        _, k0 = window(r)
        vw = qkv_ref[pl.ds(k0, wr * W), 2 * NA_WIDTH:3 * NA_WIDTH]
        pv = _dot(e_ref[...], vw) * d_ref[...]
        hm = hm_ref[...]
        out = pv[0:W] * hm[0:1]
        for h in range(1, NA_HEADS):
            out = out + pv[h * W:(h + 1) * W] * hm[h:h + 1]
        yc_ref[pl.ds(t0, W), :] = out.astype(yc_ref.dtype)

        pc = p_ref[pl.ds(t0, W), :]
        tp = pl.multiple_of(jnp.maximum(r - 1, 0) * W, W)
        tn = pl.multiple_of(jnp.minimum(r + 1, rows - 1) * W, W)
        pp = p_ref[pl.ds(tp, W), :]
        pn = p_ref[pl.ds(tn, W), :]
        pp = jnp.where(r > 0, pp, jnp.zeros_like(pp))
        pn = jnp.where(r < rows - 1, pn, jnp.zeros_like(pn))
        p3 = jnp.concatenate([pp, pc, pn], axis=0)
        wall = _dot(a_ref[...], p3)
        wsum = wall[0:W] * hm[0:1]
        for g in range(1, POOL_GROUPS):
            wsum = wsum + wall[g * W:(g + 1) * W] * hm[g:g + 1]
        t = t0 + lax.broadcasted_iota(jnp.int32, (W, POOL_WIDTH), 0)
        half = half_ref[...]
        cnt = (jnp.minimum(t + half, seq_len) - jnp.maximum(t - half, 0)).astype(f32)
        d = (wsum / cnt - pc.astype(f32)).astype(bf16)
        yb_ref[pl.ds(t0, W), :] = (_dot(d, pw_ref[...]) * ps_ref[...]).astype(yb_ref.dtype)

    scores(0, s0_ref)
    softmax(s0_ref, e0_ref, d0_ref)
    scores(1, s1_ref)

    def body(j, carry):
        r = 2 * j
        finish(r, e0_ref, d0_ref)
        softmax(s1_ref, e1_ref, d1_ref)
        scores(r + 2, s0_ref)
        finish(r + 1, e1_ref, d1_ref)
        softmax(s0_ref, e0_ref, d0_ref)
        scores(r + 3, s1_ref)
        return carry

    lax.fori_loop(0, rows // 2, body, 0)


def _napool_constants():
    W = GRID_W
    hk = np.arange(NA_WIDTH) // NA_HD
    hm = (np.arange(NA_HEADS)[:, None] == hk[None, :]).astype(np.float32)
    a = np.zeros((POOL_GROUPS, W, 3 * W), np.float32)
    t = np.arange(W)[:, None]
    j = np.arange(3 * W)[None, :] - W
    for g, w in enumerate(POOL_WINDOWS):
        a[g] = ((j >= t - w // 2) & (j < t + w - w // 2)).astype(np.float32)
    half = np.repeat(np.array([w // 2 for w in POOL_WINDOWS], np.int32), POOL_GC)[None, :]
    a = a.reshape(POOL_GROUPS * W, 3 * W)
    return dict(hm=jnp.asarray(hm, f32), a=jnp.asarray(a, bf16), half=jnp.asarray(half, jnp.int32))


def _na_bias_table(rpb):
    W, R, C = GRID_W, NA_WIN_R, NA_WIN_C
    cols = np.arange(W)
    c0 = np.clip(cols - C // 2, 0, W - C)
    valid = (cols[None, :] >= c0[:, None]) & (cols[None, :] < c0[:, None] + C)
    row_sel = np.zeros((R, R, 2 * R - 1), np.float32)
    d, kr = np.meshgrid(np.arange(R), np.arange(R), indexing="ij")
    row_sel[d, kr, kr - d + (R - 1)] = 1.0
    col_sel = np.zeros((W, W, 2 * C - 1), np.float32)
    q, kc = np.nonzero(valid)
    col_sel[q, kc, kc - q + (C - 1)] = 1.0
    tab = jnp.einsum("dkr,hrc,qjc->dhqkj", row_sel, rpb * LOG2E, col_sel,
                     precision=lax.Precision.HIGHEST)
    tab = jnp.where(jnp.asarray(valid)[None, None, :, None, :], tab, NEG_BIG)
    return tab.reshape(R, NA_HEADS * W, R * W).astype(f32)


def _napool(qkv, p, bias, mkk, pw, ps, cst):
    B, T, _ = qkv.shape
    rows = T // GRID_W
    assert rows >= NA_WIN_R and rows % 2 == 0
    seq = lambda w: pl.BlockSpec((None, T, w), lambda b: (b, 0, 0))
    consts = (bias, mkk, cst["hm"], cst["a"], cst["half"], pw, ps)
    return pl.pallas_call(
        functools.partial(_napool_kernel, rows=rows, seq_len=T),
        grid=(B,),
        in_specs=[seq(3 * NA_WIDTH), seq(POOL_WIDTH)] + [_const_spec(c.shape) for c in consts],
        out_specs=[seq(NA_WIDTH), seq(POOL_WIDTH)],
        out_shape=[jax.ShapeDtypeStruct((B, T, NA_WIDTH), bf16),
                   jax.ShapeDtypeStruct((B, T, POOL_WIDTH), bf16)],
        scratch_shapes=[pltpu.VMEM((NA_HEADS * GRID_W, NA_WIN_R * GRID_W), f32)] * 2
        + [pltpu.VMEM((NA_HEADS * GRID_W, NA_WIN_R * GRID_W), bf16)] * 2
        + [pltpu.VMEM((NA_HEADS * GRID_W, 1), f32)] * 2,
        compiler_params=_params(("parallel",)),
        name="napool",
    )(qkv, p, *consts)


def _merge_kernel(x_ref, on_ref, og_ref, yb_ref, yc_ref, g_ref, wa_ref, wb_ref, wc_ref, wo_ref,
                  lg_ref, lb_ref, o_ref):
    D = D_MODEL
    one = jnp.ones((), bf16)
    for rows in _row_pieces(x_ref.shape[0]):
        h = og_ref[rows, :]
        a_in = on_ref[rows, :] * (h * jnp.tanh(h) + h)
        m = ((one + jnp.tanh(g_ref[rows, 0:D])) * _dot(a_in, wa_ref[...]).astype(bf16)
             + (one + jnp.tanh(g_ref[rows, D:2 * D])) * _dot(yb_ref[rows, :], wb_ref[...]).astype(bf16)
             + (one + jnp.tanh(g_ref[rows, 2 * D:3 * D])) * _dot(yc_ref[rows, :], wc_ref[...]).astype(bf16))
        y = DEEPNORM_ALPHA * x_ref[rows, :] + _dot(m, wo_ref[...])
        o_ref[rows, :] = _layer_norm(y, lg_ref[...], lb_ref[...])


def _merge(x, on, og, yb, yc, g, wa, wb, wc, wo, lg, lb, tm):
    B, T, D = x.shape
    tok = lambda w: pl.BlockSpec((None, tm, w), lambda b, i: (b, i, 0))
    consts = (wa, wb, wc, wo, lg, lb)
    return pl.pallas_call(
        _merge_kernel,
        grid=(B, T // tm),
        in_specs=[tok(D), tok(GLA_VAL), tok(GLA_VAL), tok(POOL_WIDTH), tok(NA_WIDTH),
                  tok(N_BRANCH * D)] + [_const_spec(c.shape) for c in consts],
        out_specs=tok(D),
        out_shape=jax.ShapeDtypeStruct((B, T, D), f32),
        compiler_params=_params(("parallel", "parallel")),
        name="merge",
    )(x, on, og, yb, yc, g, *consts)


def _ffn_kernel(x_ref, wg_ref, wu_ref, wd_ref, lg_ref, lb_ref, o_ref):
    for rows in _row_pieces(x_ref.shape[0]):
        x = x_ref[rows, :]
        xb = x.astype(bf16)
        hg = _dot(xb, wg_ref[...])
        hu = _dot(xb, wu_ref[...])
        act = (_silu(hg) * hu).astype(bf16)
        y = DEEPNORM_ALPHA * x + _dot(act, wd_ref[...])
        o_ref[rows, :] = _layer_norm(y, lg_ref[...], lb_ref[...])


def _ffn(x, wg, wu, wd, lg, lb, tm):
    B, T, D = x.shape
    tok = pl.BlockSpec((None, tm, D), lambda b, i: (b, i, 0))
    consts = (wg, wu, wd, lg, lb)
    return pl.pallas_call(
        _ffn_kernel,
        grid=(B, T // tm),
        in_specs=[tok] + [_const_spec(c.shape) for c in consts],
        out_specs=tok,
        out_shape=jax.ShapeDtypeStruct((B, T, D), f32),
        compiler_params=_params(("parallel", "parallel")),
        name="ffn",
    )(x, *consts)


def _prepare_layer(l, w_in, gla_up_f, gla_up_b, gla_bias_f, gla_bias_b, gla_norm, pool_w, pool_scale,
                   na_rpb, w_br_a, w_br_b, w_br_c, w_out, ln1_g, ln1_b, w_gate, w_up, w_down,
                   ln2_g, ln2_b):
    wi = w_in[l]
    o_lr = 2 * GLA_KEY + 2 * GLA_VAL
    o_p = o_lr + 2 * GLA_RANK
    o_g = o_p + POOL_WIDTH + 3 * NA_WIDTH
    wlr = jnp.zeros((D_MODEL, LR_PAD), f32).at[:, :2 * GLA_RANK].set(wi[:, o_lr:o_p])
    u = jnp.zeros((LR_PAD, 2 * GLA_KEY), f32)
    u = u.at[:GLA_RANK, :GLA_KEY].set(gla_up_f[l]).at[GLA_RANK:2 * GLA_RANK, GLA_KEY:].set(gla_up_b[l])
    row = lambda a: a.reshape(1, -1).astype(f32)
    o_og = 2 * GLA_KEY + GLA_VAL
    wa = jnp.concatenate([wi[:, :o_og], 0.5 * wi[:, o_og:o_lr]], axis=1)
    return dict(
        wa=wa.astype(bf16), wlr=wlr.astype(bf16), wb=wi[:, o_p:o_g].astype(bf16),
        wg=(0.5 * wi[:, o_g:]).astype(bf16),
        u=u.astype(bf16), ub=row(jnp.concatenate([gla_bias_f[l], gla_bias_b[l]])),
        ng=row(gla_norm[l]),
        pw=jax.scipy.linalg.block_diag(*[pool_w[l, g] for g in range(POOL_GROUPS)]).astype(bf16),
        ps=row(pool_scale[l]), bias=_na_bias_table(na_rpb[l]),
        bra=(0.5 * w_br_a[l]).astype(bf16), brb=(0.5 * w_br_b[l]).astype(bf16),
        brc=(0.5 * w_br_c[l]).astype(bf16),
        wo=w_out[l].astype(bf16), ln1g=row(ln1_g[l]), ln1b=row(ln1_b[l]),
        wgate=w_gate[l].astype(bf16), wup=w_up[l].astype(bf16), wdown=w_down[l].astype(bf16),
        ln2g=row(ln2_g[l]), ln2b=row(ln2_b[l]))


def _trunk(x, layers, in_cst, gla_cst, np_cst):
    for p in layers:
        gq, v, og, dec, pin, qkv, gates = _inproj(x, p["wa"], p["wlr"], p["wb"], p["wg"], p["u"], p["ub"],
                                                  in_cst, INPROJ_TILE)
        on = _gla(gq, v, dec, p["ng"], gla_cst)
        yc, yb = _napool(qkv, pin, p["bias"], gla_cst["mkk"], p["pw"], p["ps"], np_cst)
        x = _merge(x, on, og, yb, yc, gates, p["bra"], p["brb"], p["brc"], p["wo"],
                   p["ln1g"], p["ln1b"], DENSE_TILE)
        x = _ffn(x, p["wgate"], p["wup"], p["wdown"], p["ln2g"], p["ln2b"], DENSE_TILE)
    return x


def kernel(x_prompt, x_sample, w_in, gla_up_f, gla_up_b, gla_bias_f, gla_bias_b, gla_norm, pool_w, pool_scale, na_rpb, w_br_a, w_br_b, w_br_c, w_out, ln1_g, ln1_b, w_gate, w_up, w_down, ln2_g, ln2_b):
    weights = (w_in, gla_up_f, gla_up_b, gla_bias_f, gla_bias_b, gla_norm, pool_w, pool_scale, na_rpb,
               w_br_a, w_br_b, w_br_c, w_out, ln1_g, ln1_b, w_gate, w_up, w_down, ln2_g, ln2_b)
    layers = [_prepare_layer(l, *weights) for l in range(w_in.shape[0])]
    in_cst = _inproj_constants()
    gla_cst = _gla_constants()
    np_cst = _napool_constants()
    return (_trunk(x_prompt, layers, in_cst, gla_cst, np_cst),
            _trunk(x_sample, layers, in_cst, gla_cst, np_cst))
```

```python
import functools

import numpy as np
import jax
import jax.numpy as jnp
from jax import lax
from jax.experimental import pallas as pl
from jax.experimental.pallas import tpu as pltpu

f32 = jnp.float32
bf16 = jnp.bfloat16

D_MODEL = 1024
DEPTH = 4
GRID_W = 64
GLA_HEADS = 4
GLA_DK = 64
GLA_DV = 128
GLA_KEY = GLA_HEADS * GLA_DK
GLA_VAL = GLA_HEADS * GLA_DV
GLA_RANK = 16
GLA_TAU = 16.0
GLA_CHUNK = 64
GLA_SUPER = 8
CUMSUM_BLOCK = 256
POOL_GROUPS = 4
POOL_GC = 64
POOL_WIDTH = POOL_GROUPS * POOL_GC
POOL_WINDOWS = (2, 4, 8, 16)
NA_HEADS = 4
NA_HD = 64
NA_WIDTH = NA_HEADS * NA_HD
NA_WIN_R = 8
NA_WIN_C = 16
N_BRANCH = 3
D_FF = 2816
DEEPNORM_ALPHA = (2 * DEPTH) ** 0.25
LN_EPS = 1e-5
RMS_EPS = 1e-6
LR_PAD = 128
NEG_BIG = -1e30
LOG2E = 1.4426950408889634

DENSE_TILE = 1024
ROW_PIECE = 256
INPROJ_TILE = 512

VMEM_LIMIT_BYTES = 56 * 1024 * 1024

_NN = (((1,), (0,)), ((), ()))
_NT = (((1,), (1,)), ((), ()))
_TN = (((0,), (0,)), ((), ()))


def _dot(a, b, dims=_NN):
    return lax.dot_general(a, b, dims, preferred_element_type=f32)


def _const_spec(shape):
    nd = len(shape)
    return pl.BlockSpec(shape, lambda *_: (0,) * nd, pipeline_mode=pl.Buffered(1))


def _params(sem):
    return pltpu.CompilerParams(dimension_semantics=sem, vmem_limit_bytes=VMEM_LIMIT_BYTES)


def _layer_norm(y, g, b):
    mu = jnp.mean(y, axis=-1, keepdims=True)
    yc = y - mu
    var = jnp.mean(yc * yc, axis=-1, keepdims=True)
    return yc * lax.rsqrt(var + LN_EPS) * g + b


def _row_pieces(tm):
    return [pl.ds(r, ROW_PIECE) for r in range(0, tm, ROW_PIECE)]


def _silu(x):
    h = 0.5 * x
    return h * jnp.tanh(h) + h


def _gla_decays(lr, u_ref, ub_ref, tril_ref, triu_ref):
    L = GLA_CHUNK
    tm = lr.shape[0]
    z = _dot(lr.astype(bf16), u_ref[...]) + ub_ref[...]
    la = (jnp.minimum(z, 0.0) - jnp.log(1.0 + jnp.exp(-jnp.abs(z)))) * (1.0 / GLA_TAU)
    hi = la.astype(bf16)
    lo = (la - hi.astype(f32)).astype(bf16)
    out = []
    for d, (tri_ref, last) in enumerate(((tril_ref, L - 1), (triu_ref, 0))):
        cols = slice(d * GLA_KEY, (d + 1) * GLA_KEY)
        tri = tri_ref[...]
        R = tri.shape[0]
        b = jnp.concatenate(
            [_dot(tri, hi[r:r + R, cols]) + _dot(tri, lo[r:r + R, cols]) for r in range(0, tm, R)],
            axis=0)
        ends = b.reshape(tm // L, L, GLA_KEY)[:, last, :]
        out.append((b, ends))
    return out


def _inproj_kernel(x_ref, wa_ref, wlr_ref, wb_ref, wg_ref, u_ref, ub_ref, tril_ref, triu_ref,
                   gq_ref, v_ref, og_ref, dec_ref, p_ref, qkv_ref, g_ref):
    xb = x_ref[...].astype(bf16)
    ra = _dot(xb, wa_ref[...])
    v_ref[...] = ra[:, 2 * GLA_KEY:2 * GLA_KEY + GLA_VAL].astype(bf16)
    og_ref[...] = ra[:, 2 * GLA_KEY + GLA_VAL:].astype(bf16)
    lr = _dot(xb, wlr_ref[...])
    g_ref[...] = _dot(xb, wg_ref[...]).astype(bf16)
    q = ra[:, :GLA_KEY] * (GLA_DK ** -0.5)
    k = ra[:, GLA_KEY:2 * GLA_KEY]
    for d, (b, ends) in enumerate(_gla_decays(lr, u_ref, ub_ref, tril_ref, triu_ref)):
        o = 2 * d * GLA_KEY
        gq_ref[:, o:o + GLA_KEY] = (q * jnp.exp(b)).astype(bf16)
        gq_ref[:, o + GLA_KEY:o + 2 * GLA_KEY] = (k * jnp.exp(-b)).astype(bf16)
        dec_ref[:, d * GLA_KEY:(d + 1) * GLA_KEY] = jnp.exp(ends)
    rb = _dot(xb, wb_ref[...])
    p_ref[...] = rb[:, :POOL_WIDTH].astype(bf16)
    qkv_ref[...] = rb[:, POOL_WIDTH:].astype(bf16)


def _inproj(x, wa, wlr, wb, wg, u, ub, cst, tm):
    B, T, D = x.shape
    assert tm % CUMSUM_BLOCK == 0
    n_chunks = tm // GLA_CHUNK
    widths = (4 * GLA_KEY, GLA_VAL, GLA_VAL, None, POOL_WIDTH, 3 * NA_WIDTH, N_BRANCH * D_MODEL)
    tok = lambda w: pl.BlockSpec((None, tm, w), lambda b, i: (b, i, 0))
    dec_spec = pl.BlockSpec((None, n_chunks, 2 * GLA_KEY), lambda b, i: (b, i, 0))
    consts = (wa, wlr, wb, wg, u, ub, cst["tril"], cst["triu"])
    return pl.pallas_call(
        _inproj_kernel,
        grid=(B, T // tm),
        in_specs=[tok(D)] + [_const_spec(c.shape) for c in consts],
        out_specs=[dec_spec if w is None else tok(w) for w in widths],
        out_shape=[jax.ShapeDtypeStruct((B, T // GLA_CHUNK, 2 * GLA_KEY), f32) if w is None
                   else jax.ShapeDtypeStruct((B, T, w), bf16) for w in widths],
        compiler_params=_params(("parallel", "parallel")),
        name="inproj",
    )(x, *consts)


def _inproj_constants():
    L, n = GLA_CHUNK, CUMSUM_BLOCK // GLA_CHUNK
    i = np.arange(L)
    tril = (i[None, :] <= i[:, None]).astype(np.float32)
    eye = np.eye(n, dtype=np.float32)
    return dict(tril=jnp.asarray(np.kron(eye, tril), bf16), triu=jnp.asarray(np.kron(eye, tril.T), bf16))


def _gla_super(sc, gq_ref, v_ref, dec_ref, cm_ref, mkk_ref, mkv_ref, hm_ref, st_ref, fwd):
    L, NS, H = GLA_CHUNK, GLA_SUPER, GLA_HEADS
    R = L * NS
    r0 = pl.multiple_of(sc * R, R)
    o = 0 if fwd else 2 * GLA_KEY
    q_dec = gq_ref[pl.ds(r0, R), o:o + GLA_KEY]
    k_inv = gq_ref[pl.ds(r0, R), o + GLA_KEY:o + 2 * GLA_KEY]
    v = v_ref[pl.ds(r0, R), :]
    dcol = 0 if fwd else GLA_KEY
    hm = hm_ref[...]
    state = [st_ref[p] for p in range(H // 2)]
    outs = [None] * NS
    for c in (range(NS) if fwd else range(NS - 1, -1, -1)):
        rows = slice(c * L, (c + 1) * L)
        qc, kic, vc = q_dec[rows], k_inv[rows], v[rows]
        dec = dec_ref[pl.ds(sc * NS + c, 1), dcol:dcol + GLA_KEY]
        kec = (kic.astype(f32) * dec).astype(bf16)
        kb = jnp.concatenate([kic] * H, axis=0) * mkk_ref[...]
        att = _dot(qc, kb, _NT) * cm_ref[...]
        vb = jnp.concatenate([vc] * H, axis=0) * mkv_ref[...]
        o_c = _dot(att.astype(bf16), vb)
        inter = []
        for p in range(H // 2):
            lanes = slice(p * 2 * GLA_DK, (p + 1) * 2 * GLA_DK)
            q2 = jnp.concatenate([qc[:, lanes] * hm[0:1], qc[:, lanes] * hm[1:2]], axis=0)
            oi = _dot(q2, state[p].astype(bf16), _NT)
            inter += [oi[0:L], oi[L:2 * L]]
            ke2 = jnp.concatenate([kec[:, lanes] * hm[0:1], kec[:, lanes] * hm[1:2]], axis=0)
            v2 = jnp.concatenate([vc[:, (2 * p) * GLA_DV:(2 * p + 1) * GLA_DV],
                                  vc[:, (2 * p + 1) * GLA_DV:(2 * p + 2) * GLA_DV]], axis=0)
            state[p] = dec[:, lanes] * state[p] + _dot(v2, ke2, _TN)
        outs[c] = o_c + jnp.concatenate(inter, axis=1)
    for p in range(H // 2):
        st_ref[p] = state[p]
    return jnp.concatenate(outs, axis=0)


def _gla_program(gq_ref, v_ref, dec_ref, ng_ref, cml_ref, cmu_ref, mkk_ref, mkv_ref, hm_ref,
                 o_ref, acc_ref, sf_ref, sb_ref, n_super):
    R = GLA_CHUNK * GLA_SUPER
    shared = (mkk_ref, mkv_ref, hm_ref)

    def init():
        sf_ref[...] = jnp.zeros_like(sf_ref)
        sb_ref[...] = jnp.zeros_like(sb_ref)

    def both(i):
        cf, cb = i, n_super - 1 - i
        of = _gla_super(cf, gq_ref, v_ref, dec_ref, cml_ref, *shared, sf_ref, True)
        ob = _gla_super(cb, gq_ref, v_ref, dec_ref, cmu_ref, *shared, sb_ref, False)
        return cf, cb, of, ob

    def first_half(i):
        cf, cb, of, ob = both(i)
        acc_ref[pl.ds(pl.multiple_of(cf * R, R), R), :] = of
        acc_ref[pl.ds(pl.multiple_of(cb * R, R), R), :] = ob

    def finish(c, o_new):
        r0 = pl.multiple_of(c * R, R)
        o = acc_ref[pl.ds(r0, R), :] + o_new
        outs = []
        for h in range(GLA_HEADS):
            oh = o[:, h * GLA_DV:(h + 1) * GLA_DV]
            ms = jnp.mean(oh * oh, axis=-1, keepdims=True)
            outs.append(oh * lax.rsqrt(ms + RMS_EPS) * ng_ref[...])
        o_ref[pl.ds(r0, R), :] = jnp.concatenate(outs, axis=1).astype(o_ref.dtype)

    def second_half(i):
        cf, cb, of, ob = both(i)
        finish(cf, of)
        finish(cb, ob)

    return init, first_half, second_half


def _gla_constants():
    L, H = GLA_CHUNK, GLA_HEADS
    i = np.arange(L)
    tril = (i[None, :] <= i[:, None]).astype(np.float32)
    hk = np.arange(H * L) // L
    hv = np.arange(H * GLA_DV) // GLA_DV
    mkk = (hk[:, None] == hk[None, :]).astype(np.float32)
    mkv = (hk[:, None] == hv[None, :]).astype(np.float32)
    half = (np.arange(2)[:, None] == (np.arange(2 * GLA_DK) // GLA_DK)[None, :]).astype(np.float32)
    return dict(
        cml=jnp.asarray(np.tile(tril, (1, H)), f32), cmu=jnp.asarray(np.tile(tril.T, (1, H)), f32),
        mkk=jnp.asarray(mkk, bf16), mkv=jnp.asarray(mkv, bf16), hm=jnp.asarray(half, bf16))


def _napool_program(qkv_ref, p_ref, bias_ref, mkk_ref, hm_ref, a_ref, half_ref, pw_ref, ps_ref,
                    yc_ref, yb_ref, s0_ref, s1_ref, e0_ref, e1_ref, d0_ref, d1_ref, rows, seq_len):
    W = GRID_W
    wr = NA_WIN_R

    def window(r):
        r0 = jnp.clip(r - wr // 2, 0, rows - wr)
        return r0, pl.multiple_of(r0 * W, W)

    def scores(r, s_ref):
        r = jnp.minimum(r, rows - 1)
        t0 = pl.multiple_of(r * W, W)
        r0, k0 = window(r)
        q = (qkv_ref[pl.ds(t0, W), 0:NA_WIDTH].astype(f32) * (NA_HD ** -0.5 * LOG2E)).astype(bf16)
        kw = qkv_ref[pl.ds(k0, wr * W), NA_WIDTH:2 * NA_WIDTH]
        qs = jnp.concatenate([q] * NA_HEADS, axis=0) * mkk_ref[...]
        s_ref[...] = _dot(qs, kw, _NT) + bias_ref[r - r0]

    def softmax(s_ref, e_ref, d_ref):
        s = s_ref[...]
        e = jnp.exp2(s - jnp.max(s, axis=-1, keepdims=True))
        d_ref[...] = 1.0 / jnp.sum(e, axis=-1, keepdims=True)
        e_ref[...] = e.astype(bf16)

    def finish(r, e_ref, d_ref):
        t0 = pl.multiple_of(r * W, W)
        _, k0 = window(r)
        vw = qkv_ref[pl.ds(k0, wr * W), 2 * NA_WIDTH:3 * NA_WIDTH]
        pv = _dot(e_ref[...], vw) * d_ref[...]
        hm = hm_ref[...]
        out = pv[0:W] * hm[0:1]
        for h in range(1, NA_HEADS):
            out = out + pv[h * W:(h + 1) * W] * hm[h:h + 1]
        yc_ref[pl.ds(t0, W), :] = out.astype(yc_ref.dtype)

        pc = p_ref[pl.ds(t0, W), :]
        tp = pl.multiple_of(jnp.maximum(r - 1, 0) * W, W)
        tn = pl.multiple_of(jnp.minimum(r + 1, rows - 1) * W, W)
        pp = p_ref[pl.ds(tp, W), :]
        pn = p_ref[pl.ds(tn, W), :]
        pp = jnp.where(r > 0, pp, jnp.zeros_like(pp))
        pn = jnp.where(r < rows - 1, pn, jnp.zeros_like(pn))
        p3 = jnp.concatenate([pp, pc, pn], axis=0)
        wall = _dot(a_ref[...], p3)
        wsum = wall[0:W] * hm[0:1]
        for g in range(1, POOL_GROUPS):
            wsum = wsum + wall[g * W:(g + 1) * W] * hm[g:g + 1]
        t = t0 + lax.broadcasted_iota(jnp.int32, (W, POOL_WIDTH), 0)
        half = half_ref[...]
        cnt = (jnp.minimum(t + half, seq_len) - jnp.maximum(t - half, 0)).astype(f32)
        d = (wsum / cnt - pc.astype(f32)).astype(bf16)
        yb_ref[pl.ds(t0, W), :] = (_dot(d, pw_ref[...]) * ps_ref[...]).astype(yb_ref.dtype)

    def prologue():
        scores(0, s0_ref)
        softmax(s0_ref, e0_ref, d0_ref)
        scores(1, s1_ref)

    def step(j):
        r = 2 * j
        finish(r, e0_ref, d0_ref)
        softmax(s1_ref, e1_ref, d1_ref)
        scores(r + 2, s0_ref)
        finish(r + 1, e1_ref, d1_ref)
        softmax(s0_ref, e0_ref, d0_ref)
        scores(r + 3, s1_ref)

    return prologue, step


def _napool_constants():
    W = GRID_W
    hk = np.arange(NA_WIDTH) // NA_HD
    hm = (np.arange(NA_HEADS)[:, None] == hk[None, :]).astype(np.float32)
    a = np.zeros((POOL_GROUPS, W, 3 * W), np.float32)
    t = np.arange(W)[:, None]
    j = np.arange(3 * W)[None, :] - W
    for g, w in enumerate(POOL_WINDOWS):
        a[g] = ((j >= t - w // 2) & (j < t + w - w // 2)).astype(np.float32)
    half = np.repeat(np.array([w // 2 for w in POOL_WINDOWS], np.int32), POOL_GC)[None, :]
    a = a.reshape(POOL_GROUPS * W, 3 * W)
    return dict(hm=jnp.asarray(hm, f32), a=jnp.asarray(a, bf16), half=jnp.asarray(half, jnp.int32))


def _na_bias_table(rpb):
    W, R, C = GRID_W, NA_WIN_R, NA_WIN_C
    cols = np.arange(W)
    c0 = np.clip(cols - C // 2, 0, W - C)
    valid = (cols[None, :] >= c0[:, None]) & (cols[None, :] < c0[:, None] + C)
    row_sel = np.zeros((R, R, 2 * R - 1), np.float32)
    d, kr = np.meshgrid(np.arange(R), np.arange(R), indexing="ij")
    row_sel[d, kr, kr - d + (R - 1)] = 1.0
    col_sel = np.zeros((W, W, 2 * C - 1), np.float32)
    q, kc = np.nonzero(valid)
    col_sel[q, kc, kc - q + (C - 1)] = 1.0
    tab = jnp.einsum("dkr,hrc,qjc->dhqkj", row_sel, rpb * LOG2E, col_sel,
                     precision=lax.Precision.HIGHEST)
    tab = jnp.where(jnp.asarray(valid)[None, None, :, None, :], tab, NEG_BIG)
    return tab.reshape(R, NA_HEADS * W, R * W).astype(f32)


def _step_loop(lo, hi, step):
    def body(i, carry):
        step(i)
        return carry
    lax.fori_loop(lo, hi, body, 0)


def _gla_kernel(*refs, n_super):
    init, first_half, second_half = _gla_program(*refs, n_super)
    init()
    _step_loop(0, n_super // 2, first_half)
    _step_loop(n_super // 2, n_super, second_half)


def _gla(gq, v, dec, ng, cst):
    B, T, _ = gq.shape
    R = GLA_CHUNK * GLA_SUPER
    assert T % (2 * R) == 0
    seq = lambda a: pl.BlockSpec((None,) + a.shape[1:], lambda b: (b, 0, 0))
    consts = (ng, cst["cml"], cst["cmu"], cst["mkk"], cst["mkv"], cst["hm"])
    state = pltpu.VMEM((GLA_HEADS // 2, GLA_DV, 2 * GLA_DK), f32)
    return pl.pallas_call(
        functools.partial(_gla_kernel, n_super=T // R),
        grid=(B,),
        in_specs=[seq(gq), seq(v), seq(dec)] + [_const_spec(c.shape) for c in consts],
        out_specs=pl.BlockSpec((None, T, GLA_VAL), lambda b: (b, 0, 0)),
        out_shape=jax.ShapeDtypeStruct((B, T, GLA_VAL), bf16),
        scratch_shapes=[pltpu.VMEM((T, GLA_VAL), f32), state, state],
        compiler_params=_params(("parallel",)),
        name="gla",
    )(gq, v, dec, *consts)


def _napool_kernel(*refs, rows, seq_len):
    prologue, step = _napool_program(*refs, rows, seq_len)
    prologue()
    _step_loop(0, rows // 2, step)


def _napool(qkv, p, bias, mkk, pw, ps, cst):
    B, T, _ = qkv.shape
    rows = T // GRID_W
    assert rows >= NA_WIN_R and rows % 2 == 0
    seq = lambda w: pl.BlockSpec((None, T, w), lambda b: (b, 0, 0))
    consts = (bias, mkk, cst["hm"], cst["a"], cst["half"], pw, ps)
    sc = (NA_HEADS * GRID_W, NA_WIN_R * GRID_W)
    return pl.pallas_call(
        functools.partial(_napool_kernel, rows=rows, seq_len=T),
        grid=(B,),
        in_specs=[seq(3 * NA_WIDTH), seq(POOL_WIDTH)] + [_const_spec(c.shape) for c in consts],
        out_specs=[seq(NA_WIDTH), seq(POOL_WIDTH)],
        out_shape=[jax.ShapeDtypeStruct((B, T, NA_WIDTH), bf16),
                   jax.ShapeDtypeStruct((B, T, POOL_WIDTH), bf16)],
        scratch_shapes=[pltpu.VMEM(sc, f32), pltpu.VMEM(sc, f32), pltpu.VMEM(sc, bf16), pltpu.VMEM(sc, bf16),
                        pltpu.VMEM((sc[0], 1), f32), pltpu.VMEM((sc[0], 1), f32)],
        compiler_params=_params(("parallel",)),
        name="napool",
    )(qkv, p, *consts)


def _merge_kernel(x_ref, on_ref, og_ref, yb_ref, yc_ref, g_ref, wa_ref, wb_ref, wc_ref, wo_ref,
                  lg_ref, lb_ref, o_ref):
    D = D_MODEL
    one = jnp.ones((), bf16)
    for rows in _row_pieces(x_ref.shape[0]):
        h = og_ref[rows, :]
        a_in = on_ref[rows, :] * (h * jnp.tanh(h) + h)
        m = ((one + jnp.tanh(g_ref[rows, 0:D])) * _dot(a_in, wa_ref[...]).astype(bf16)
             + (one + jnp.tanh(g_ref[rows, D:2 * D])) * _dot(yb_ref[rows, :], wb_ref[...]).astype(bf16)
             + (one + jnp.tanh(g_ref[rows, 2 * D:3 * D])) * _dot(yc_ref[rows, :], wc_ref[...]).astype(bf16))
        y = DEEPNORM_ALPHA * x_ref[rows, :] + _dot(m, wo_ref[...])
        o_ref[rows, :] = _layer_norm(y, lg_ref[...], lb_ref[...])


def _merge(x, on, og, yb, yc, g, wa, wb, wc, wo, lg, lb, tm):
    B, T, D = x.shape
    tok = lambda w: pl.BlockSpec((None, tm, w), lambda b, i: (b, i, 0))
    consts = (wa, wb, wc, wo, lg, lb)
    return pl.pallas_call(
        _merge_kernel,
        grid=(B, T // tm),
        in_specs=[tok(D), tok(GLA_VAL), tok(GLA_VAL), tok(POOL_WIDTH), tok(NA_WIDTH),
                  tok(N_BRANCH * D)] + [_const_spec(c.shape) for c in consts],
        out_specs=tok(D),
        out_shape=jax.ShapeDtypeStruct((B, T, D), f32),
        compiler_params=_params(("parallel", "parallel")),
        name="merge",
    )(x, on, og, yb, yc, g, *consts)


def _ffn_kernel(x_ref, wg_ref, wu_ref, wd_ref, lg_ref, lb_ref, o_ref):
    for rows in _row_pieces(x_ref.shape[0]):
        x = x_ref[rows, :]
        xb = x.astype(bf16)
        hg = _dot(xb, wg_ref[...])
        hu = _dot(xb, wu_ref[...])
        act = (_silu(hg) * hu).astype(bf16)
        y = DEEPNORM_ALPHA * x + _dot(act, wd_ref[...])
        o_ref[rows, :] = _layer_norm(y, lg_ref[...], lb_ref[...])


def _ffn(x, wg, wu, wd, lg, lb, tm):
    B, T, D = x.shape
    tok = pl.BlockSpec((None, tm, D), lambda b, i: (b, i, 0))
    consts = (wg, wu, wd, lg, lb)
    return pl.pallas_call(
        _ffn_kernel,
        grid=(B, T // tm),
        in_specs=[tok] + [_const_spec(c.shape) for c in consts],
        out_specs=tok,
        out_shape=jax.ShapeDtypeStruct((B, T, D), f32),
        compiler_params=_params(("parallel", "parallel")),
        name="ffn",
    )(x, *consts)


def _prepare_layer(l, w_in, gla_up_f, gla_up_b, gla_bias_f, gla_bias_b, gla_norm, pool_w, pool_scale,
                   na_rpb, w_br_a, w_br_b, w_br_c, w_out, ln1_g, ln1_b, w_gate, w_up, w_down,
                   ln2_g, ln2_b):
    wi = w_in[l]
    o_lr = 2 * GLA_KEY + 2 * GLA_VAL
    o_p = o_lr + 2 * GLA_RANK
    o_g = o_p + POOL_WIDTH + 3 * NA_WIDTH
    wlr = jnp.zeros((D_MODEL, LR_PAD), f32).at[:, :2 * GLA_RANK].set(wi[:, o_lr:o_p])
    u = jnp.zeros((LR_PAD, 2 * GLA_KEY), f32)
    u = u.at[:GLA_RANK, :GLA_KEY].set(gla_up_f[l]).at[GLA_RANK:2 * GLA_RANK, GLA_KEY:].set(gla_up_b[l])
    row = lambda a: a.reshape(1, -1).astype(f32)
    o_og = 2 * GLA_KEY + GLA_VAL
    wa = jnp.concatenate([wi[:, :o_og], 0.5 * wi[:, o_og:o_lr]], axis=1)
    return dict(
        wa=wa.astype(bf16), wlr=wlr.astype(bf16), wb=wi[:, o_p:o_g].astype(bf16),
        wg=(0.5 * wi[:, o_g:]).astype(bf16),
        u=u.astype(bf16), ub=row(jnp.concatenate([gla_bias_f[l], gla_bias_b[l]])),
        ng=row(gla_norm[l]),
        pw=jax.scipy.linalg.block_diag(*[pool_w[l, g] for g in range(POOL_GROUPS)]).astype(bf16),
        ps=row(pool_scale[l]), bias=_na_bias_table(na_rpb[l]),
        bra=(0.5 * w_br_a[l]).astype(bf16), brb=(0.5 * w_br_b[l]).astype(bf16),
        brc=(0.5 * w_br_c[l]).astype(bf16),
        wo=w_out[l].astype(bf16), ln1g=row(ln1_g[l]), ln1b=row(ln1_b[l]),
        wgate=w_gate[l].astype(bf16), wup=w_up[l].astype(bf16), wdown=w_down[l].astype(bf16),
        ln2g=row(ln2_g[l]), ln2b=row(ln2_b[l]))


def _trunk(x, layers, in_cst, gla_cst, np_cst):
    for p in layers:
        gq, v, og, dec, pin, qkv, gates = _inproj(x, p["wa"], p["wlr"], p["wb"], p["wg"], p["u"], p["ub"],
                                                  in_cst, INPROJ_TILE)
        on = _gla(gq, v, dec, p["ng"], gla_cst)
        yc, yb = _napool(qkv, pin, p["bias"], gla_cst["mkk"], p["pw"], p["ps"], np_cst)
        x = _merge(x, on, og, yb, yc, gates, p["bra"], p["brb"], p["brc"], p["wo"],
                   p["ln1g"], p["ln1b"], DENSE_TILE)
        x = _ffn(x, p["wgate"], p["wup"], p["wdown"], p["ln2g"], p["ln2b"], DENSE_TILE)
    return x


def kernel(x_prompt, x_sample, w_in, gla_up_f, gla_up_b, gla_bias_f, gla_bias_b, gla_norm, pool_w, pool_scale, na_rpb, w_br_a, w_br_b, w_br_c, w_out, ln1_g, ln1_b, w_gate, w_up, w_down, ln2_g, ln2_b):
    weights = (w_in, gla_up_f, gla_up_b, gla_bias_f, gla_bias_b, gla_norm, pool_w, pool_scale, na_rpb,
               w_br_a, w_br_b, w_br_c, w_out, ln1_g, ln1_b, w_gate, w_up, w_down, ln2_g, ln2_b)
    layers = [_prepare_layer(l, *weights) for l in range(w_in.shape[0])]
    in_cst = _inproj_constants()
    gla_cst = _gla_constants()
    np_cst = _napool_constants()
    return (_trunk(x_prompt, layers, in_cst, gla_cst, np_cst),
            _trunk(x_sample, layers, in_cst, gla_cst, np_cst))
```

```python
import functools

import numpy as np
import jax
import jax.numpy as jnp
from jax import lax
from jax.experimental import pallas as pl
from jax.experimental.pallas import tpu as pltpu

f32 = jnp.float32
bf16 = jnp.bfloat16

D_MODEL = 1024
DEPTH = 4
GRID_W = 64
GLA_HEADS = 4
GLA_DK = 64
GLA_DV = 128
GLA_KEY = GLA_HEADS * GLA_DK
GLA_VAL = GLA_HEADS * GLA_DV
GLA_RANK = 16
GLA_TAU = 16.0
GLA_CHUNK = 64
GLA_SUPER = 8
CUMSUM_BLOCK = 256
POOL_GROUPS = 4
POOL_GC = 64
POOL_WIDTH = POOL_GROUPS * POOL_GC
POOL_WINDOWS = (2, 4, 8, 16)
NA_HEADS = 4
NA_HD = 64
NA_WIDTH = NA_HEADS * NA_HD
NA_WIN_R = 8
NA_WIN_C = 16
N_BRANCH = 3
D_FF = 2816
DEEPNORM_ALPHA = (2 * DEPTH) ** 0.25
LN_EPS = 1e-5
RMS_EPS = 1e-6
LR_PAD = 128
NEG_BIG = -1e30
LOG2E = 1.4426950408889634

DENSE_TILE = 1024
ROW_PIECE = 256
INPROJ_TILE = 512

VMEM_LIMIT_BYTES = 56 * 1024 * 1024

_NN = (((1,), (0,)), ((), ()))
_NT = (((1,), (1,)), ((), ()))
_TN = (((0,), (0,)), ((), ()))


def _dot(a, b, dims=_NN):
    return lax.dot_general(a, b, dims, preferred_element_type=f32)


def _const_spec(shape):
    nd = len(shape)
    return pl.BlockSpec(shape, lambda *_: (0,) * nd, pipeline_mode=pl.Buffered(1))


def _params(sem):
    return pltpu.CompilerParams(dimension_semantics=sem, vmem_limit_bytes=VMEM_LIMIT_BYTES)


def _layer_norm(y, g, b):
    mu = jnp.mean(y, axis=-1, keepdims=True)
    yc = y - mu
    var = jnp.mean(yc * yc, axis=-1, keepdims=True)
    return yc * lax.rsqrt(var + LN_EPS) * g + b


def _row_pieces(tm):
    return [pl.ds(r, ROW_PIECE) for r in range(0, tm, ROW_PIECE)]


def _silu(x):
    h = 0.5 * x
    return h * jnp.tanh(h) + h


def _gla_decays(lr, u_ref, ub_ref, tril_ref, triu_ref):
    L = GLA_CHUNK
    tm = lr.shape[0]
    z = _dot(lr.astype(bf16), u_ref[...]) + ub_ref[...]
    la = (jnp.minimum(z, 0.0) - jnp.log(1.0 + jnp.exp(-jnp.abs(z)))) * (1.0 / GLA_TAU)
    hi = la.astype(bf16)
    lo = (la - hi.astype(f32)).astype(bf16)
    out = []
    for d, (tri_ref, last) in enumerate(((tril_ref, L - 1), (triu_ref, 0))):
        cols = slice(d * GLA_KEY, (d + 1) * GLA_KEY)
        tri = tri_ref[...]
        R = tri.shape[0]
        b = jnp.concatenate(
            [_dot(tri, hi[r:r + R, cols]) + _dot(tri, lo[r:r + R, cols]) for r in range(0, tm, R)],
            axis=0)
        ends = b.reshape(tm // L, L, GLA_KEY)[:, last, :]
        out.append((b, ends))
    return out


def _inproj_kernel(x_ref, wa_ref, wlr_ref, wb_ref, wg_ref, u_ref, ub_ref, tril_ref, triu_ref,
                   gq_ref, v_ref, og_ref, dec_ref, p_ref, qkv_ref, g_ref):
    xb = x_ref[...].astype(bf16)
    ra = _dot(xb, wa_ref[...])
    v_ref[...] = ra[:, 2 * GLA_KEY:2 * GLA_KEY + GLA_VAL].astype(bf16)
    og_ref[...] = ra[:, 2 * GLA_KEY + GLA_VAL:].astype(bf16)
    lr = _dot(xb, wlr_ref[...])
    g_ref[...] = _dot(xb, wg_ref[...]).astype(bf16)
    q = ra[:, :GLA_KEY]
    k = ra[:, GLA_KEY:2 * GLA_KEY]
    for d, (b, ends) in enumerate(_gla_decays(lr, u_ref, ub_ref, tril_ref, triu_ref)):
        o = 2 * d * GLA_KEY
        gq_ref[:, o:o + GLA_KEY] = (q * jnp.exp(b)).astype(bf16)
        gq_ref[:, o + GLA_KEY:o + 2 * GLA_KEY] = (k * jnp.exp(-b)).astype(bf16)
        dec_ref[:, d * GLA_KEY:(d + 1) * GLA_KEY] = jnp.exp(ends)
    rb = _dot(xb, wb_ref[...])
    p_ref[...] = rb[:, :POOL_WIDTH].astype(bf16)
    qkv_ref[...] = rb[:, POOL_WIDTH:].astype(bf16)


def _inproj(x, wa, wlr, wb, wg, u, ub, cst, tm):
    B, T, D = x.shape
    assert tm % CUMSUM_BLOCK == 0
    n_chunks = tm // GLA_CHUNK
    widths = (4 * GLA_KEY, GLA_VAL, GLA_VAL, None, POOL_WIDTH, 3 * NA_WIDTH, N_BRANCH * D_MODEL)
    tok = lambda w: pl.BlockSpec((None, tm, w), lambda b, i: (b, i, 0))
    dec_spec = pl.BlockSpec((None, n_chunks, 2 * GLA_KEY), lambda b, i: (b, i, 0))
    consts = (wa, wlr, wb, wg, u, ub, cst["tril"], cst["triu"])
    return pl.pallas_call(
        _inproj_kernel,
        grid=(B, T // tm),
        in_specs=[tok(D)] + [_const_spec(c.shape) for c in consts],
        out_specs=[dec_spec if w is None else tok(w) for w in widths],
        out_shape=[jax.ShapeDtypeStruct((B, T // GLA_CHUNK, 2 * GLA_KEY), f32) if w is None
                   else jax.ShapeDtypeStruct((B, T, w), bf16) for w in widths],
        compiler_params=_params(("parallel", "parallel")),
        name="inproj",
    )(x, *consts)


def _inproj_constants():
    L, n = GLA_CHUNK, CUMSUM_BLOCK // GLA_CHUNK
    i = np.arange(L)
    tril = (i[None, :] <= i[:, None]).astype(np.float32)
    eye = np.eye(n, dtype=np.float32)
    return dict(tril=jnp.asarray(np.kron(eye, tril), bf16), triu=jnp.asarray(np.kron(eye, tril.T), bf16))


def _gla_inter(qc, kic, vc, dec, state, hm):
    L, H = GLA_CHUNK, GLA_HEADS
    kec = (kic.astype(f32) * dec).astype(bf16)
    inter = []
    for p in range(H // 2):
        lanes = slice(p * 2 * GLA_DK, (p + 1) * 2 * GLA_DK)
        q2 = jnp.concatenate([qc[:, lanes] * hm[0:1], qc[:, lanes] * hm[1:2]], axis=0)
        oi = _dot(q2, state[p].astype(bf16), _NT)
        inter += [oi[0:L], oi[L:2 * L]]
        ke2 = jnp.concatenate([kec[:, lanes] * hm[0:1], kec[:, lanes] * hm[1:2]], axis=0)
        v2 = jnp.concatenate([vc[:, (2 * p) * GLA_DV:(2 * p + 1) * GLA_DV],
                              vc[:, (2 * p + 1) * GLA_DV:(2 * p + 2) * GLA_DV]], axis=0)
        state[p] = dec[:, lanes] * state[p] + _dot(v2, ke2, _TN)
    return jnp.concatenate(inter, axis=1)


def _gla_super(sc, gq_ref, v_ref, dec_ref, cml_ref, cmu_ref, mkk_ref, mkv_ref, hm_ref, st_ref, fwd):
    L, NS, H = GLA_CHUNK, GLA_SUPER, GLA_HEADS
    R = L * NS
    r0 = pl.multiple_of(sc * R, R)
    o = 0 if fwd else 2 * GLA_KEY
    q_dec = gq_ref[pl.ds(r0, R), o:o + GLA_KEY]
    k_inv = gq_ref[pl.ds(r0, R), o + GLA_KEY:o + 2 * GLA_KEY]
    v = v_ref[pl.ds(r0, R), :]
    if fwd:
        q_oth = gq_ref[pl.ds(r0, R), 2 * GLA_KEY:3 * GLA_KEY]
        k_oth = gq_ref[pl.ds(r0, R), 3 * GLA_KEY:4 * GLA_KEY]
    dcol = 0 if fwd else GLA_KEY
    hm = hm_ref[...]
    state = [st_ref[p] for p in range(H // 2)]
    outs = [None] * NS
    for c in (range(NS) if fwd else range(NS - 1, -1, -1)):
        rows = slice(c * L, (c + 1) * L)
        qc, kic, vc = q_dec[rows], k_inv[rows], v[rows]
        dec = dec_ref[pl.ds(sc * NS + c, 1), dcol:dcol + GLA_KEY]
        o_c = _gla_inter(qc, kic, vc, dec, state, hm)
        if fwd:
            kb = jnp.concatenate([kic] * H, axis=0) * mkk_ref[...]
            att_f = _dot(qc, kb, _NT) * cml_ref[...]
            kb = jnp.concatenate([k_oth[rows]] * H, axis=0) * mkk_ref[...]
            att_b = _dot(q_oth[rows], kb, _NT) * cmu_ref[...]
            vb = jnp.concatenate([vc] * H, axis=0) * mkv_ref[...]
            o2 = _dot(jnp.concatenate([att_f, att_b], axis=0).astype(bf16), vb)
            o_c = o_c + (o2[0:L] + o2[L:2 * L])
        outs[c] = o_c
    for p in range(H // 2):
        st_ref[p] = state[p]
    return jnp.concatenate(outs, axis=0)


def _gla_program(gq_ref, v_ref, dec_ref, ng_ref, cml_ref, cmu_ref, mkk_ref, mkv_ref, hm_ref,
                 o_ref, acc_ref, sf_ref, sb_ref, n_super):
    R = GLA_CHUNK * GLA_SUPER
    shared = (mkk_ref, mkv_ref, hm_ref)

    def init():
        sf_ref[...] = jnp.zeros_like(sf_ref)
        sb_ref[...] = jnp.zeros_like(sb_ref)

    def both(i):
        cf, cb = i, n_super - 1 - i
        of = _gla_super(cf, gq_ref, v_ref, dec_ref, cml_ref, cmu_ref, *shared, sf_ref, True)
        ob = _gla_super(cb, gq_ref, v_ref, dec_ref, cml_ref, cmu_ref, *shared, sb_ref, False)
        return cf, cb, of, ob

    def first_half(i):
        cf, cb, of, ob = both(i)
        acc_ref[pl.ds(pl.multiple_of(cf * R, R), R), :] = of
        acc_ref[pl.ds(pl.multiple_of(cb * R, R), R), :] = ob

    def finish(c, o_new):
        r0 = pl.multiple_of(c * R, R)
        o = acc_ref[pl.ds(r0, R), :] + o_new
        outs = []
        for h in range(GLA_HEADS):
            oh = o[:, h * GLA_DV:(h + 1) * GLA_DV]
            ms = jnp.mean(oh * oh, axis=-1, keepdims=True)
            outs.append(oh * lax.rsqrt(ms + RMS_EPS) * ng_ref[...])
        o_ref[pl.ds(r0, R), :] = jnp.concatenate(outs, axis=1).astype(o_ref.dtype)

    def second_half(i):
        cf, cb, of, ob = both(i)
        finish(cf, of)
        finish(cb, ob)

    return init, first_half, second_half


def _gla_constants():
    L, H = GLA_CHUNK, GLA_HEADS
    i = np.arange(L)
    tril = (i[None, :] <= i[:, None]).astype(np.float32)
    hk = np.arange(H * L) // L
    hv = np.arange(H * GLA_DV) // GLA_DV
    mkk = (hk[:, None] == hk[None, :]).astype(np.float32)
    mkv = (hk[:, None] == hv[None, :]).astype(np.float32)
    half = (np.arange(2)[:, None] == (np.arange(2 * GLA_DK) // GLA_DK)[None, :]).astype(np.float32)
    return dict(
        cml=jnp.asarray(np.tile(tril, (1, H)), f32), cmu=jnp.asarray(np.tile(tril.T, (1, H)), f32),
        mkk=jnp.asarray(mkk, bf16), mkv=jnp.asarray(mkv, bf16), hm=jnp.asarray(half, bf16))


def _napool_program(qkv_ref, p_ref, bias_ref, mkk_ref, hm_ref, a_ref, half_ref, pw_ref, ps_ref,
                    yc_ref, yb_ref, s0_ref, s1_ref, e0_ref, e1_ref, d0_ref, d1_ref, rows, seq_len):
    W = GRID_W
    wr = NA_WIN_R

    def window(r):
        r0 = jnp.clip(r - wr // 2, 0, rows - wr)
        return r0, pl.multiple_of(r0 * W, W)

    def scores(r, s_ref):
        r = jnp.minimum(r, rows - 1)
        t0 = pl.multiple_of(r * W, W)
        r0, k0 = window(r)
        q = qkv_ref[pl.ds(t0, W), 0:NA_WIDTH]
        kw = qkv_ref[pl.ds(k0, wr * W), NA_WIDTH:2 * NA_WIDTH]
        qs = jnp.concatenate([q] * NA_HEADS, axis=0) * mkk_ref[...]
        s_ref[...] = _dot(qs, kw, _NT) + bias_ref[r - r0]

    def softmax(s_ref, e_ref, d_ref):
        s = s_ref[...]
        e = jnp.exp2(s - jnp.max(s, axis=-1, keepdims=True))
        d_ref[...] = 1.0 / jnp.sum(e, axis=-1, keepdims=True)
        e_ref[...] = e.astype(bf16)

    def finish(r, e_ref, d_ref):
        t0 = pl.multiple_of(r * W, W)
        _, k0 = window(r)
        vw = qkv_ref[pl.ds(k0, wr * W), 2 * NA_WIDTH:3 * NA_WIDTH]
        pv = _dot(e_ref[...], vw) * d_ref[...]
        hm = hm_ref[...]
        out = pv[0:W] * hm[0:1]
        for h in range(1, NA_HEADS):
            out = out + pv[h * W:(h + 1) * W] * hm[h:h + 1]
        yc_ref[pl.ds(t0, W), :] = out.astype(yc_ref.dtype)

        pc = p_ref[pl.ds(t0, W), :]
        tp = pl.multiple_of(jnp.maximum(r - 1, 0) * W, W)
        tn = pl.multiple_of(jnp.minimum(r + 1, rows - 1) * W, W)
        pp = p_ref[pl.ds(tp, W), :]
        pn = p_ref[pl.ds(tn, W), :]
        pp = jnp.where(r > 0, pp, jnp.zeros_like(pp))
        pn = jnp.where(r < rows - 1, pn, jnp.zeros_like(pn))
        p3 = jnp.concatenate([pp, pc, pn], axis=0)
        wall = _dot(a_ref[...], p3)
        wsum = wall[0:W] * hm[0:1]
        for g in range(1, POOL_GROUPS):
            wsum = wsum + wall[g * W:(g + 1) * W] * hm[g:g + 1]
        t = t0 + lax.broadcasted_iota(jnp.int32, (W, POOL_WIDTH), 0)
        half = half_ref[...]
        cnt = (jnp.minimum(t + half, seq_len) - jnp.maximum(t - half, 0)).astype(f32)
        d = (wsum / cnt - pc.astype(f32)).astype(bf16)
        yb_ref[pl.ds(t0, W), :] = (_dot(d, pw_ref[...]) * ps_ref[...]).astype(yb_ref.dtype)

    def prologue():
        scores(0, s0_ref)
        softmax(s0_ref, e0_ref, d0_ref)
        scores(1, s1_ref)

    def step(j):
        r = 2 * j
        finish(r, e0_ref, d0_ref)
        softmax(s1_ref, e1_ref, d1_ref)
        scores(r + 2, s0_ref)
        finish(r + 1, e1_ref, d1_ref)
        softmax(s0_ref, e0_ref, d0_ref)
        scores(r + 3, s1_ref)

    return prologue, step


def _napool_constants():
    W = GRID_W
    hk = np.arange(NA_WIDTH) // NA_HD
    hm = (np.arange(NA_HEADS)[:, None] == hk[None, :]).astype(np.float32)
    a = np.zeros((POOL_GROUPS, W, 3 * W), np.float32)
    t = np.arange(W)[:, None]
    j = np.arange(3 * W)[None, :] - W
    for g, w in enumerate(POOL_WINDOWS):
        a[g] = ((j >= t - w // 2) & (j < t + w - w // 2)).astype(np.float32)
    half = np.repeat(np.array([w // 2 for w in POOL_WINDOWS], np.int32), POOL_GC)[None, :]
    a = a.reshape(POOL_GROUPS * W, 3 * W)
    return dict(hm=jnp.asarray(hm, f32), a=jnp.asarray(a, bf16), half=jnp.asarray(half, jnp.int32))


def _na_bias_table(rpb):
    W, R, C = GRID_W, NA_WIN_R, NA_WIN_C
    cols = np.arange(W)
    c0 = np.clip(cols - C // 2, 0, W - C)
    valid = (cols[None, :] >= c0[:, None]) & (cols[None, :] < c0[:, None] + C)
    row_sel = np.zeros((R, R, 2 * R - 1), np.float32)
    d, kr = np.meshgrid(np.arange(R), np.arange(R), indexing="ij")
    row_sel[d, kr, kr - d + (R - 1)] = 1.0
    col_sel = np.zeros((W, W, 2 * C - 1), np.float32)
    q, kc = np.nonzero(valid)
    col_sel[q, kc, kc - q + (C - 1)] = 1.0
    tab = jnp.einsum("dkr,hrc,qjc->dhqkj", row_sel, rpb * LOG2E, col_sel,
                     precision=lax.Precision.HIGHEST)
    tab = jnp.where(jnp.asarray(valid)[None, None, :, None, :], tab, NEG_BIG)
    return tab.reshape(R, NA_HEADS * W, R * W).astype(f32)


def _step_loop(lo, hi, step):
    def body(i, carry):
        step(i)
        return carry
    lax.fori_loop(lo, hi, body, 0)


def _gla_kernel(*refs, n_super):
    init, first_half, second_half = _gla_program(*refs, n_super)
    init()
    _step_loop(0, n_super // 2, first_half)
    _step_loop(n_super // 2, n_super, second_half)


def _gla(gq, v, dec, ng, cst):
    B, T, _ = gq.shape
    R = GLA_CHUNK * GLA_SUPER
    assert T % (2 * R) == 0
    seq = lambda a: pl.BlockSpec((None,) + a.shape[1:], lambda b: (b, 0, 0))
    consts = (ng, cst["cml"], cst["cmu"], cst["mkk"], cst["mkv"], cst["hm"])
    state = pltpu.VMEM((GLA_HEADS // 2, GLA_DV, 2 * GLA_DK), f32)
    return pl.pallas_call(
        functools.partial(_gla_kernel, n_super=T // R),
        grid=(B,),
        in_specs=[seq(gq), seq(v), seq(dec)] + [_const_spec(c.shape) for c in consts],
        out_specs=pl.BlockSpec((None, T, GLA_VAL), lambda b: (b, 0, 0)),
        out_shape=jax.ShapeDtypeStruct((B, T, GLA_VAL), bf16),
        scratch_shapes=[pltpu.VMEM((T, GLA_VAL), f32), state, state],
        compiler_params=_params(("parallel",)),
        name="gla",
    )(gq, v, dec, *consts)


def _napool_kernel(*refs, rows, seq_len):
    prologue, step = _napool_program(*refs, rows, seq_len)
    prologue()
    _step_loop(0, rows // 2, step)


def _napool(qkv, p, bias, mkk, pw, ps, cst):
    B, T, _ = qkv.shape
    rows = T // GRID_W
    assert rows >= NA_WIN_R and rows % 2 == 0
    seq = lambda w: pl.BlockSpec((None, T, w), lambda b: (b, 0, 0))
    consts = (bias, mkk, cst["hm"], cst["a"], cst["half"], pw, ps)
    sc = (NA_HEADS * GRID_W, NA_WIN_R * GRID_W)
    return pl.pallas_call(
        functools.partial(_napool_kernel, rows=rows, seq_len=T),
        grid=(B,),
        in_specs=[seq(3 * NA_WIDTH), seq(POOL_WIDTH)] + [_const_spec(c.shape) for c in consts],
        out_specs=[seq(NA_WIDTH), seq(POOL_WIDTH)],
        out_shape=[jax.ShapeDtypeStruct((B, T, NA_WIDTH), bf16),
                   jax.ShapeDtypeStruct((B, T, POOL_WIDTH), bf16)],
        scratch_shapes=[pltpu.VMEM(sc, f32), pltpu.VMEM(sc, f32), pltpu.VMEM(sc, bf16), pltpu.VMEM(sc, bf16),
                        pltpu.VMEM((sc[0], 1), f32), pltpu.VMEM((sc[0], 1), f32)],
        compiler_params=_params(("parallel",)),
        name="napool",
    )(qkv, p, *consts)


def _merge_kernel(x_ref, on_ref, og_ref, yb_ref, yc_ref, g_ref, wa_ref, wb_ref, wc_ref, wo_ref,
                  lg_ref, lb_ref, o_ref):
    D = D_MODEL
    one = jnp.ones((), bf16)
    for rows in _row_pieces(x_ref.shape[0]):
        h = og_ref[rows, :]
        a_in = on_ref[rows, :] * (h * jnp.tanh(h) + h)
        m = ((one + jnp.tanh(g_ref[rows, 0:D])) * _dot(a_in, wa_ref[...]).astype(bf16)
             + (one + jnp.tanh(g_ref[rows, D:2 * D])) * _dot(yb_ref[rows, :], wb_ref[...]).astype(bf16)
             + (one + jnp.tanh(g_ref[rows, 2 * D:3 * D])) * _dot(yc_ref[rows, :], wc_ref[...]).astype(bf16))
        y = DEEPNORM_ALPHA * x_ref[rows, :] + _dot(m, wo_ref[...])
        o_ref[rows, :] = _layer_norm(y, lg_ref[...], lb_ref[...])


def _merge(x, on, og, yb, yc, g, wa, wb, wc, wo, lg, lb, tm):
    B, T, D = x.shape
    tok = lambda w: pl.BlockSpec((None, tm, w), lambda b, i: (b, i, 0))
    consts = (wa, wb, wc, wo, lg, lb)
    return pl.pallas_call(
        _merge_kernel,
        grid=(B, T // tm),
        in_specs=[tok(D), tok(GLA_VAL), tok(GLA_VAL), tok(POOL_WIDTH), tok(NA_WIDTH),
                  tok(N_BRANCH * D)] + [_const_spec(c.shape) for c in consts],
        out_specs=tok(D),
        out_shape=jax.ShapeDtypeStruct((B, T, D), f32),
        compiler_params=_params(("parallel", "parallel")),
        name="merge",
    )(x, on, og, yb, yc, g, *consts)


def _ffn_kernel(x_ref, wg_ref, wu_ref, wd_ref, lg_ref, lb_ref, o_ref):
    for rows in _row_pieces(x_ref.shape[0]):
        x = x_ref[rows, :]
        xb = x.astype(bf16)
        hg = _dot(xb, wg_ref[...])
        hu = _dot(xb, wu_ref[...])
        act = (_silu(hg) * hu).astype(bf16)
        y = DEEPNORM_ALPHA * x + _dot(act, wd_ref[...])
        o_ref[rows, :] = _layer_norm(y, lg_ref[...], lb_ref[...])


def _ffn(x, wg, wu, wd, lg, lb, tm):
    B, T, D = x.shape
    tok = pl.BlockSpec((None, tm, D), lambda b, i: (b, i, 0))
    consts = (wg, wu, wd, lg, lb)
    return pl.pallas_call(
        _ffn_kernel,
        grid=(B, T // tm),
        in_specs=[tok] + [_const_spec(c.shape) for c in consts],
        out_specs=tok,
        out_shape=jax.ShapeDtypeStruct((B, T, D), f32),
        compiler_params=_params(("parallel", "parallel")),
        name="ffn",
    )(x, *consts)


def _prepare_layer(l, w_in, gla_up_f, gla_up_b, gla_bias_f, gla_bias_b, gla_norm, pool_w, pool_scale,
                   na_rpb, w_br_a, w_br_b, w_br_c, w_out, ln1_g, ln1_b, w_gate, w_up, w_down,
                   ln2_g, ln2_b):
    wi = w_in[l]
    o_lr = 2 * GLA_KEY + 2 * GLA_VAL
    o_p = o_lr + 2 * GLA_RANK
    o_g = o_p + POOL_WIDTH + 3 * NA_WIDTH
    wlr = jnp.zeros((D_MODEL, LR_PAD), f32).at[:, :2 * GLA_RANK].set(wi[:, o_lr:o_p])
    u = jnp.zeros((LR_PAD, 2 * GLA_KEY), f32)
    u = u.at[:GLA_RANK, :GLA_KEY].set(gla_up_f[l]).at[GLA_RANK:2 * GLA_RANK, GLA_KEY:].set(gla_up_b[l])
    row = lambda a: a.reshape(1, -1).astype(f32)
    o_og = 2 * GLA_KEY + GLA_VAL
    o_cq = o_p + POOL_WIDTH
    wb = jnp.concatenate([wi[:, o_p:o_cq], (NA_HD ** -0.5 * LOG2E) * wi[:, o_cq:o_cq + NA_WIDTH],
                          wi[:, o_cq + NA_WIDTH:o_g]], axis=1)
    wa = jnp.concatenate([(GLA_DK ** -0.5) * wi[:, :GLA_KEY], wi[:, GLA_KEY:o_og], 0.5 * wi[:, o_og:o_lr]], axis=1)
    return dict(
        wa=wa.astype(bf16), wlr=wlr.astype(bf16), wb=wb.astype(bf16),
        wg=(0.5 * wi[:, o_g:]).astype(bf16),
        u=u.astype(bf16), ub=row(jnp.concatenate([gla_bias_f[l], gla_bias_b[l]])),
        ng=row(gla_norm[l]),
        pw=jax.scipy.linalg.block_diag(*[pool_w[l, g] for g in range(POOL_GROUPS)]).astype(bf16),
        ps=row(pool_scale[l]), bias=_na_bias_table(na_rpb[l]),
        bra=(0.5 * w_br_a[l]).astype(bf16), brb=(0.5 * w_br_b[l]).astype(bf16),
        brc=(0.5 * w_br_c[l]).astype(bf16),
        wo=w_out[l].astype(bf16), ln1g=row(ln1_g[l]), ln1b=row(ln1_b[l]),
        wgate=w_gate[l].astype(bf16), wup=w_up[l].astype(bf16), wdown=w_down[l].astype(bf16),
        ln2g=row(ln2_g[l]), ln2b=row(ln2_b[l]))


def _trunk(x, layers, in_cst, gla_cst, np_cst):
    for p in layers:
        gq, v, og, dec, pin, qkv, gates = _inproj(x, p["wa"], p["wlr"], p["wb"], p["wg"], p["u"], p["ub"],
                                                  in_cst, INPROJ_TILE)
        on = _gla(gq, v, dec, p["ng"], gla_cst)
        yc, yb = _napool(qkv, pin, p["bias"], gla_cst["mkk"], p["pw"], p["ps"], np_cst)
        x = _merge(x, on, og, yb, yc, gates, p["bra"], p["brb"], p["brc"], p["wo"],
                   p["ln1g"], p["ln1b"], DENSE_TILE)
        x = _ffn(x, p["wgate"], p["wup"], p["wdown"], p["ln2g"], p["ln2b"], DENSE_TILE)
    return x


def kernel(x_prompt, x_sample, w_in, gla_up_f, gla_up_b, gla_bias_f, gla_bias_b, gla_norm, pool_w, pool_scale, na_rpb, w_br_a, w_br_b, w_br_c, w_out, ln1_g, ln1_b, w_gate, w_up, w_down, ln2_g, ln2_b):
    weights = (w_in, gla_up_f, gla_up_b, gla_bias_f, gla_bias_b, gla_norm, pool_w, pool_scale, na_rpb,
               w_br_a, w_br_b, w_br_c, w_out, ln1_g, ln1_b, w_gate, w_up, w_down, ln2_g, ln2_b)
    layers = [_prepare_layer(l, *weights) for l in range(w_in.shape[0])]
    in_cst = _inproj_constants()
    gla_cst = _gla_constants()
    np_cst = _napool_constants()
    return (_trunk(x_prompt, layers, in_cst, gla_cst, np_cst),
            _trunk(x_sample, layers, in_cst, gla_cst, np_cst))
```

```python
import functools

import numpy as np
import jax
import jax.numpy as jnp
from jax import lax
from jax.experimental import pallas as pl
from jax.experimental.pallas import tpu as pltpu

f32 = jnp.float32
bf16 = jnp.bfloat16

D_MODEL = 1024
DEPTH = 4
GRID_W = 64
GLA_HEADS = 4
GLA_DK = 64
GLA_DV = 128
GLA_KEY = GLA_HEADS * GLA_DK
GLA_VAL = GLA_HEADS * GLA_DV
GLA_RANK = 16
GLA_TAU = 16.0
GLA_CHUNK = 64
GLA_SUPER = 8
CUMSUM_BLOCK = 256
POOL_GROUPS = 4
POOL_GC = 64
POOL_WIDTH = POOL_GROUPS * POOL_GC
POOL_WINDOWS = (2, 4, 8, 16)
NA_HEADS = 4
NA_HD = 64
NA_WIDTH = NA_HEADS * NA_HD
NA_WIN_R = 8
NA_WIN_C = 16
N_BRANCH = 3
D_FF = 2816
DEEPNORM_ALPHA = (2 * DEPTH) ** 0.25
LN_EPS = 1e-5
RMS_EPS = 1e-6
LR_PAD = 128
NEG_BIG = -1e30
LOG2E = 1.4426950408889634

DENSE_TILE = 1024
ROW_PIECE = 256
INPROJ_TILE = 512

VMEM_LIMIT_BYTES = 56 * 1024 * 1024

_NN = (((1,), (0,)), ((), ()))
_NT = (((1,), (1,)), ((), ()))
_TN = (((0,), (0,)), ((), ()))


def _dot(a, b, dims=_NN):
    return lax.dot_general(a, b, dims, preferred_element_type=f32)


def _const_spec(shape):
    nd = len(shape)
    return pl.BlockSpec(shape, lambda *_: (0,) * nd, pipeline_mode=pl.Buffered(1))


def _params(sem):
    return pltpu.CompilerParams(dimension_semantics=sem, vmem_limit_bytes=VMEM_LIMIT_BYTES)


def _layer_norm(y, g, b):
    mu = jnp.mean(y, axis=-1, keepdims=True)
    yc = y - mu
    var = jnp.mean(yc * yc, axis=-1, keepdims=True)
    return yc * lax.rsqrt(var + LN_EPS) * g + b


def _row_pieces(tm):
    return [pl.ds(r, ROW_PIECE) for r in range(0, tm, ROW_PIECE)]


def _silu(x):
    h = 0.5 * x
    return h * jnp.tanh(h) + h


def _gla_decays(lr, u_ref, ub_ref, tril_ref, triu_ref):
    L = GLA_CHUNK
    tm = lr.shape[0]
    z = _dot(lr.astype(bf16), u_ref[...]) + ub_ref[...]
    la = (jnp.minimum(z, 0.0) - jnp.log(1.0 + jnp.exp(-jnp.abs(z)))) * (1.0 / GLA_TAU)
    hi = la.astype(bf16)
    lo = (la - hi.astype(f32)).astype(bf16)
    out = []
    for d, (tri_ref, last) in enumerate(((tril_ref, L - 1), (triu_ref, 0))):
        cols = slice(d * GLA_KEY, (d + 1) * GLA_KEY)
        tri = tri_ref[...]
        R = tri.shape[0]
        b = jnp.concatenate(
            [_dot(tri, hi[r:r + R, cols]) + _dot(tri, lo[r:r + R, cols]) for r in range(0, tm, R)],
            axis=0)
        ends = b.reshape(tm // L, L, GLA_KEY)[:, last, :]
        out.append((b, ends))
    return out


def _inproj_kernel(x_ref, wa_ref, wlr_ref, wb_ref, wg_ref, u_ref, ub_ref, tril_ref, triu_ref,
                   gq_ref, v_ref, og_ref, dec_ref, p_ref, qkv_ref, g_ref):
    xb = x_ref[...].astype(bf16)
    ra = _dot(xb, wa_ref[...])
    v_ref[...] = ra[:, 2 * GLA_KEY:2 * GLA_KEY + GLA_VAL].astype(bf16)
    og_ref[...] = ra[:, 2 * GLA_KEY + GLA_VAL:].astype(bf16)
    lr = _dot(xb, wlr_ref[...])
    g_ref[...] = _dot(xb, wg_ref[...]).astype(bf16)
    q = ra[:, :GLA_KEY]
    k = ra[:, GLA_KEY:2 * GLA_KEY]
    for d, (b, ends) in enumerate(_gla_decays(lr, u_ref, ub_ref, tril_ref, triu_ref)):
        o = 2 * d * GLA_KEY
        gq_ref[:, o:o + GLA_KEY] = (q * jnp.exp(b)).astype(bf16)
        gq_ref[:, o + GLA_KEY:o + 2 * GLA_KEY] = (k * jnp.exp(-b)).astype(bf16)
        dec_ref[:, d * GLA_KEY:(d + 1) * GLA_KEY] = jnp.exp(ends)
    rb = _dot(xb, wb_ref[...])
    p_ref[...] = rb[:, :POOL_WIDTH].astype(bf16)
    qkv_ref[...] = rb[:, POOL_WIDTH:].astype(bf16)


def _inproj(x, wa, wlr, wb, wg, u, ub, cst, tm):
    B, T, D = x.shape
    assert tm % CUMSUM_BLOCK == 0
    n_chunks = tm // GLA_CHUNK
    widths = (4 * GLA_KEY, GLA_VAL, GLA_VAL, None, POOL_WIDTH, 3 * NA_WIDTH, N_BRANCH * D_MODEL)
    tok = lambda w: pl.BlockSpec((None, tm, w), lambda b, i: (b, i, 0))
    dec_spec = pl.BlockSpec((None, n_chunks, 2 * GLA_KEY), lambda b, i: (b, i, 0))
    consts = (wa, wlr, wb, wg, u, ub, cst["tril"], cst["triu"])
    return pl.pallas_call(
        _inproj_kernel,
        grid=(B, T // tm),
        in_specs=[tok(D)] + [_const_spec(c.shape) for c in consts],
        out_specs=[dec_spec if w is None else tok(w) for w in widths],
        out_shape=[jax.ShapeDtypeStruct((B, T // GLA_CHUNK, 2 * GLA_KEY), f32) if w is None
                   else jax.ShapeDtypeStruct((B, T, w), bf16) for w in widths],
        compiler_params=_params(("parallel", "parallel")),
        name="inproj",
    )(x, *consts)


def _inproj_constants():
    L, n = GLA_CHUNK, CUMSUM_BLOCK // GLA_CHUNK
    i = np.arange(L)
    tril = (i[None, :] <= i[:, None]).astype(np.float32)
    eye = np.eye(n, dtype=np.float32)
    return dict(tril=jnp.asarray(np.kron(eye, tril), bf16), triu=jnp.asarray(np.kron(eye, tril.T), bf16))


def _gla_inter(qc, kic, vc, dec, st, mkk):
    L, H = GLA_CHUNK, GLA_HEADS
    kec = (kic.astype(f32) * dec).astype(bf16)
    q4 = jnp.concatenate([qc] * H, axis=0) * mkk
    oi = _dot(q4, st.astype(bf16), _NT)
    inter = jnp.concatenate([oi[h * L:(h + 1) * L, (h // 2) * GLA_DV:(h // 2 + 1) * GLA_DV] for h in range(H)],
                            axis=1)
    ke4 = jnp.concatenate([kec] * H, axis=0) * mkk
    zero = jnp.zeros((L, GLA_DV), bf16)
    v4 = jnp.concatenate(
        [jnp.concatenate([vc[:, h * GLA_DV:(h + 1) * GLA_DV], zero] if h // 2 == 0 else
                         [zero, vc[:, h * GLA_DV:(h + 1) * GLA_DV]], axis=1) for h in range(H)],
        axis=0)
    return inter, dec * st + _dot(v4, ke4, _TN)


def _gla_super(sc, gq_ref, v_ref, dec_ref, cml_ref, cmu_ref, mkk_ref, mkv_ref, st_ref, fwd):
    L, NS, H = GLA_CHUNK, GLA_SUPER, GLA_HEADS
    R = L * NS
    r0 = pl.multiple_of(sc * R, R)
    o = 0 if fwd else 2 * GLA_KEY
    q_dec = gq_ref[pl.ds(r0, R), o:o + GLA_KEY]
    k_inv = gq_ref[pl.ds(r0, R), o + GLA_KEY:o + 2 * GLA_KEY]
    v = v_ref[pl.ds(r0, R), :]
    if fwd:
        q_oth = gq_ref[pl.ds(r0, R), 2 * GLA_KEY:3 * GLA_KEY]
        k_oth = gq_ref[pl.ds(r0, R), 3 * GLA_KEY:4 * GLA_KEY]
    dcol = 0 if fwd else GLA_KEY
    mkk = mkk_ref[...]
    st = st_ref[...]
    outs = [None] * NS
    for c in (range(NS) if fwd else range(NS - 1, -1, -1)):
        rows = slice(c * L, (c + 1) * L)
        qc, kic, vc = q_dec[rows], k_inv[rows], v[rows]
        dec = dec_ref[pl.ds(sc * NS + c, 1), dcol:dcol + GLA_KEY]
        o_c, st = _gla_inter(qc, kic, vc, dec, st, mkk)
        if fwd:
            kb = jnp.concatenate([kic] * H, axis=0) * mkk
            att_f = _dot(qc, kb, _NT) * cml_ref[...]
            kb = jnp.concatenate([k_oth[rows]] * H, axis=0) * mkk
            att_b = _dot(q_oth[rows], kb, _NT) * cmu_ref[...]
            vb = jnp.concatenate([vc] * H, axis=0) * mkv_ref[...]
            o2 = _dot(jnp.concatenate([att_f, att_b], axis=0).astype(bf16), vb)
            o_c = o_c + (o2[0:L] + o2[L:2 * L])
        outs[c] = o_c
    st_ref[...] = st
    return jnp.concatenate(outs, axis=0)


def _gla_program(gq_ref, v_ref, dec_ref, ng_ref, cml_ref, cmu_ref, mkk_ref, mkv_ref,
                 o_ref, acc_ref, sf_ref, sb_ref, n_super):
    R = GLA_CHUNK * GLA_SUPER
    shared = (mkk_ref, mkv_ref)

    def init():
        sf_ref[...] = jnp.zeros_like(sf_ref)
        sb_ref[...] = jnp.zeros_like(sb_ref)

    def both(i):
        cf, cb = i, n_super - 1 - i
        of = _gla_super(cf, gq_ref, v_ref, dec_ref, cml_ref, cmu_ref, *shared, sf_ref, True)
        ob = _gla_super(cb, gq_ref, v_ref, dec_ref, cml_ref, cmu_ref, *shared, sb_ref, False)
        return cf, cb, of, ob

    def first_half(i):
        cf, cb, of, ob = both(i)
        acc_ref[pl.ds(pl.multiple_of(cf * R, R), R), :] = of
        acc_ref[pl.ds(pl.multiple_of(cb * R, R), R), :] = ob

    def finish(c, o_new):
        r0 = pl.multiple_of(c * R, R)
        o = acc_ref[pl.ds(r0, R), :] + o_new
        outs = []
        for h in range(GLA_HEADS):
            oh = o[:, h * GLA_DV:(h + 1) * GLA_DV]
            ms = jnp.mean(oh * oh, axis=-1, keepdims=True)
            outs.append(oh * lax.rsqrt(ms + RMS_EPS) * ng_ref[...])
        o_ref[pl.ds(r0, R), :] = jnp.concatenate(outs, axis=1).astype(o_ref.dtype)

    def second_half(i):
        cf, cb, of, ob = both(i)
        finish(cf, of)
        finish(cb, ob)

    return init, first_half, second_half


def _gla_constants():
    L, H = GLA_CHUNK, GLA_HEADS
    i = np.arange(L)
    tril = (i[None, :] <= i[:, None]).astype(np.float32)
    hk = np.arange(H * L) // L
    hv = np.arange(H * GLA_DV) // GLA_DV
    mkk = (hk[:, None] == hk[None, :]).astype(np.float32)
    mkv = (hk[:, None] == hv[None, :]).astype(np.float32)
    return dict(
        cml=jnp.asarray(np.tile(tril, (1, H)), f32), cmu=jnp.asarray(np.tile(tril.T, (1, H)), f32),
        mkk=jnp.asarray(mkk, bf16), mkv=jnp.asarray(mkv, bf16))


def _napool_program(qkv_ref, p_ref, bias_ref, mkk_ref, hm_ref, a_ref, half_ref, pw_ref, ps_ref,
                    yc_ref, yb_ref, s0_ref, s1_ref, e0_ref, e1_ref, d0_ref, d1_ref, rows, seq_len):
    W = GRID_W
    wr = NA_WIN_R

    def window(r):
        r0 = jnp.clip(r - wr // 2, 0, rows - wr)
        return r0, pl.multiple_of(r0 * W, W)

    def scores(r, s_ref):
        r = jnp.minimum(r, rows - 1)
        t0 = pl.multiple_of(r * W, W)
        r0, k0 = window(r)
        q = qkv_ref[pl.ds(t0, W), 0:NA_WIDTH]
        kw = qkv_ref[pl.ds(k0, wr * W), NA_WIDTH:2 * NA_WIDTH]
        qs = jnp.concatenate([q] * NA_HEADS, axis=0) * mkk_ref[...]
        s_ref[...] = _dot(qs, kw, _NT) + bias_ref[r - r0]

    def softmax(s_ref, e_ref, d_ref):
        s = s_ref[...]
        e = jnp.exp2(s - jnp.max(s, axis=-1, keepdims=True))
        d_ref[...] = 1.0 / jnp.sum(e, axis=-1, keepdims=True)
        e_ref[...] = e.astype(bf16)

    def finish(r, e_ref, d_ref):
        t0 = pl.multiple_of(r * W, W)
        _, k0 = window(r)
        vw = qkv_ref[pl.ds(k0, wr * W), 2 * NA_WIDTH:3 * NA_WIDTH]
        pv = _dot(e_ref[...], vw) * d_ref[...]
        hm = hm_ref[...]
        out = pv[0:W] * hm[0:1]
        for h in range(1, NA_HEADS):
            out = out + pv[h * W:(h + 1) * W] * hm[h:h + 1]
        yc_ref[pl.ds(t0, W), :] = out.astype(yc_ref.dtype)

        pc = p_ref[pl.ds(t0, W), :]
        tp = pl.multiple_of(jnp.maximum(r - 1, 0) * W, W)
        tn = pl.multiple_of(jnp.minimum(r + 1, rows - 1) * W, W)
        pp = p_ref[pl.ds(tp, W), :]
        pn = p_ref[pl.ds(tn, W), :]
        pp = jnp.where(r > 0, pp, jnp.zeros_like(pp))
        pn = jnp.where(r < rows - 1, pn, jnp.zeros_like(pn))
        p3 = jnp.concatenate([pp, pc, pn], axis=0)
        wall = _dot(a_ref[...], p3)
        wsum = wall[0:W] * hm[0:1]
        for g in range(1, POOL_GROUPS):
            wsum = wsum + wall[g * W:(g + 1) * W] * hm[g:g + 1]
        t = t0 + lax.broadcasted_iota(jnp.int32, (W, POOL_WIDTH), 0)
        half = half_ref[...]
        cnt = (jnp.minimum(t + half, seq_len) - jnp.maximum(t - half, 0)).astype(f32)
        d = (wsum / cnt - pc.astype(f32)).astype(bf16)
        yb_ref[pl.ds(t0, W), :] = (_dot(d, pw_ref[...]) * ps_ref[...]).astype(yb_ref.dtype)

    def prologue():
        scores(0, s0_ref)
        softmax(s0_ref, e0_ref, d0_ref)
        scores(1, s1_ref)

    def step(j):
        r = 2 * j
        finish(r, e0_ref, d0_ref)
        softmax(s1_ref, e1_ref, d1_ref)
        scores(r + 2, s0_ref)
        finish(r + 1, e1_ref, d1_ref)
        softmax(s0_ref, e0_ref, d0_ref)
        scores(r + 3, s1_ref)

    return prologue, step


def _napool_constants():
    W = GRID_W
    hk = np.arange(NA_WIDTH) // NA_HD
    hm = (np.arange(NA_HEADS)[:, None] == hk[None, :]).astype(np.float32)
    a = np.zeros((POOL_GROUPS, W, 3 * W), np.float32)
    t = np.arange(W)[:, None]
    j = np.arange(3 * W)[None, :] - W
    for g, w in enumerate(POOL_WINDOWS):
        a[g] = ((j >= t - w // 2) & (j < t + w - w // 2)).astype(np.float32)
    half = np.repeat(np.array([w // 2 for w in POOL_WINDOWS], np.int32), POOL_GC)[None, :]
    a = a.reshape(POOL_GROUPS * W, 3 * W)
    return dict(hm=jnp.asarray(hm, f32), a=jnp.asarray(a, bf16), half=jnp.asarray(half, jnp.int32))


def _na_bias_table(rpb):
    W, R, C = GRID_W, NA_WIN_R, NA_WIN_C
    cols = np.arange(W)
    c0 = np.clip(cols - C // 2, 0, W - C)
    valid = (cols[None, :] >= c0[:, None]) & (cols[None, :] < c0[:, None] + C)
    row_sel = np.zeros((R, R, 2 * R - 1), np.float32)
    d, kr = np.meshgrid(np.arange(R), np.arange(R), indexing="ij")
    row_sel[d, kr, kr - d + (R - 1)] = 1.0
    col_sel = np.zeros((W, W, 2 * C - 1), np.float32)
    q, kc = np.nonzero(valid)
    col_sel[q, kc, kc - q + (C - 1)] = 1.0
    tab = jnp.einsum("dkr,hrc,qjc->dhqkj", row_sel, rpb * LOG2E, col_sel,
                     precision=lax.Precision.HIGHEST)
    tab = jnp.where(jnp.asarray(valid)[None, None, :, None, :], tab, NEG_BIG)
    return tab.reshape(R, NA_HEADS * W, R * W).astype(f32)


def _step_loop(lo, hi, step):
    def body(i, carry):
        step(i)
        return carry
    lax.fori_loop(lo, hi, body, 0)


def _gla_kernel(*refs, n_super):
    init, first_half, second_half = _gla_program(*refs, n_super)
    init()
    _step_loop(0, n_super // 2, first_half)
    _step_loop(n_super // 2, n_super, second_half)


def _gla(gq, v, dec, ng, cst):
    B, T, _ = gq.shape
    R = GLA_CHUNK * GLA_SUPER
    assert T % (2 * R) == 0
    seq = lambda a: pl.BlockSpec((None,) + a.shape[1:], lambda b: (b, 0, 0))
    consts = (ng, cst["cml"], cst["cmu"], cst["mkk"], cst["mkv"])
    state = pltpu.VMEM((GLA_HEADS // 2 * GLA_DV, GLA_KEY), f32)
    return pl.pallas_call(
        functools.partial(_gla_kernel, n_super=T // R),
        grid=(B,),
        in_specs=[seq(gq), seq(v), seq(dec)] + [_const_spec(c.shape) for c in consts],
        out_specs=pl.BlockSpec((None, T, GLA_VAL), lambda b: (b, 0, 0)),
        out_shape=jax.ShapeDtypeStruct((B, T, GLA_VAL), bf16),
        scratch_shapes=[pltpu.VMEM((T, GLA_VAL), f32), state, state],
        compiler_params=_params(("parallel",)),
        name="gla",
    )(gq, v, dec, *consts)


def _napool_kernel(*refs, rows, seq_len):
    prologue, step = _napool_program(*refs, rows, seq_len)
    prologue()
    _step_loop(0, rows // 2, step)


def _napool(qkv, p, bias, mkk, pw, ps, cst):
    B, T, _ = qkv.shape
    rows = T // GRID_W
    assert rows >= NA_WIN_R and rows % 2 == 0
    seq = lambda w: pl.BlockSpec((None, T, w), lambda b: (b, 0, 0))
    consts = (bias, mkk, cst["hm"], cst["a"], cst["half"], pw, ps)
    sc = (NA_HEADS * GRID_W, NA_WIN_R * GRID_W)
    return pl.pallas_call(
        functools.partial(_napool_kernel, rows=rows, seq_len=T),
        grid=(B,),
        in_specs=[seq(3 * NA_WIDTH), seq(POOL_WIDTH)] + [_const_spec(c.shape) for c in consts],
        out_specs=[seq(NA_WIDTH), seq(POOL_WIDTH)],
        out_shape=[jax.ShapeDtypeStruct((B, T, NA_WIDTH), bf16),
                   jax.ShapeDtypeStruct((B, T, POOL_WIDTH), bf16)],
        scratch_shapes=[pltpu.VMEM(sc, f32), pltpu.VMEM(sc, f32), pltpu.VMEM(sc, bf16), pltpu.VMEM(sc, bf16),
                        pltpu.VMEM((sc[0], 1), f32), pltpu.VMEM((sc[0], 1), f32)],
        compiler_params=_params(("parallel",)),
        name="napool",
    )(qkv, p, *consts)


def _merge_kernel(x_ref, on_ref, og_ref, yb_ref, yc_ref, g_ref, wa_ref, wb_ref, wc_ref, wo_ref,
                  lg_ref, lb_ref, o_ref):
    D = D_MODEL
    one = jnp.ones((), bf16)
    for rows in _row_pieces(x_ref.shape[0]):
        h = og_ref[rows, :]
        a_in = on_ref[rows, :] * (h * jnp.tanh(h) + h)
        m = ((one + jnp.tanh(g_ref[rows, 0:D])) * _dot(a_in, wa_ref[...]).astype(bf16)
             + (one + jnp.tanh(g_ref[rows, D:2 * D])) * _dot(yb_ref[rows, :], wb_ref[...]).astype(bf16)
             + (one + jnp.tanh(g_ref[rows, 2 * D:3 * D])) * _dot(yc_ref[rows, :], wc_ref[...]).astype(bf16))
        y = DEEPNORM_ALPHA * x_ref[rows, :] + _dot(m, wo_ref[...])
        o_ref[rows, :] = _layer_norm(y, lg_ref[...], lb_ref[...])


def _merge(x, on, og, yb, yc, g, wa, wb, wc, wo, lg, lb, tm):
    B, T, D = x.shape
    tok = lambda w: pl.BlockSpec((None, tm, w), lambda b, i: (b, i, 0))
    consts = (wa, wb, wc, wo, lg, lb)
    return pl.pallas_call(
        _merge_kernel,
        grid=(B, T // tm),
        in_specs=[tok(D), tok(GLA_VAL), tok(GLA_VAL), tok(POOL_WIDTH), tok(NA_WIDTH),
                  tok(N_BRANCH * D)] + [_const_spec(c.shape) for c in consts],
        out_specs=tok(D),
        out_shape=jax.ShapeDtypeStruct((B, T, D), f32),
        compiler_params=_params(("parallel", "parallel")),
        name="merge",
    )(x, on, og, yb, yc, g, *consts)


def _ffn_kernel(x_ref, wg_ref, wu_ref, wd_ref, lg_ref, lb_ref, o_ref):
    for rows in _row_pieces(x_ref.shape[0]):
        x = x_ref[rows, :]
        xb = x.astype(bf16)
        hg = _dot(xb, wg_ref[...])
        hu = _dot(xb, wu_ref[...])
        act = (_silu(hg) * hu).astype(bf16)
        y = DEEPNORM_ALPHA * x + _dot(act, wd_ref[...])
        o_ref[rows, :] = _layer_norm(y, lg_ref[...], lb_ref[...])


def _ffn(x, wg, wu, wd, lg, lb, tm):
    B, T, D = x.shape
    tok = pl.BlockSpec((None, tm, D), lambda b, i: (b, i, 0))
    consts = (wg, wu, wd, lg, lb)
    return pl.pallas_call(
        _ffn_kernel,
        grid=(B, T // tm),
        in_specs=[tok] + [_const_spec(c.shape) for c in consts],
        out_specs=tok,
        out_shape=jax.ShapeDtypeStruct((B, T, D), f32),
        compiler_params=_params(("parallel", "parallel")),
        name="ffn",
    )(x, *consts)


def _prepare_layer(l, w_in, gla_up_f, gla_up_b, gla_bias_f, gla_bias_b, gla_norm, pool_w, pool_scale,
                   na_rpb, w_br_a, w_br_b, w_br_c, w_out, ln1_g, ln1_b, w_gate, w_up, w_down,
                   ln2_g, ln2_b):
    wi = w_in[l]
    o_lr = 2 * GLA_KEY + 2 * GLA_VAL
    o_p = o_lr + 2 * GLA_RANK
    o_g = o_p + POOL_WIDTH + 3 * NA_WIDTH
    wlr = jnp.zeros((D_MODEL, LR_PAD), f32).at[:, :2 * GLA_RANK].set(wi[:, o_lr:o_p])
    u = jnp.zeros((LR_PAD, 2 * GLA_KEY), f32)
    u = u.at[:GLA_RANK, :GLA_KEY].set(gla_up_f[l]).at[GLA_RANK:2 * GLA_RANK, GLA_KEY:].set(gla_up_b[l])
    row = lambda a: a.reshape(1, -1).astype(f32)
    o_og = 2 * GLA_KEY + GLA_VAL
    o_cq = o_p + POOL_WIDTH
    wb = jnp.concatenate([wi[:, o_p:o_cq], (NA_HD ** -0.5 * LOG2E) * wi[:, o_cq:o_cq + NA_WIDTH],
                          wi[:, o_cq + NA_WIDTH:o_g]], axis=1)
    wa = jnp.concatenate([(GLA_DK ** -0.5) * wi[:, :GLA_KEY], wi[:, GLA_KEY:o_og], 0.5 * wi[:, o_og:o_lr]], axis=1)
    return dict(
        wa=wa.astype(bf16), wlr=wlr.astype(bf16), wb=wb.astype(bf16),
        wg=(0.5 * wi[:, o_g:]).astype(bf16),
        u=u.astype(bf16), ub=row(jnp.concatenate([gla_bias_f[l], gla_bias_b[l]])),
        ng=row(gla_norm[l]),
        pw=jax.scipy.linalg.block_diag(*[pool_w[l, g] for g in range(POOL_GROUPS)]).astype(bf16),
        ps=row(pool_scale[l]), bias=_na_bias_table(na_rpb[l]),
        bra=(0.5 * w_br_a[l]).astype(bf16), brb=(0.5 * w_br_b[l]).astype(bf16),
        brc=(0.5 * w_br_c[l]).astype(bf16),
        wo=w_out[l].astype(bf16), ln1g=row(ln1_g[l]), ln1b=row(ln1_b[l]),
        wgate=w_gate[l].astype(bf16), wup=w_up[l].astype(bf16), wdown=w_down[l].astype(bf16),
        ln2g=row(ln2_g[l]), ln2b=row(ln2_b[l]))


def _trunk(x, layers, in_cst, gla_cst, np_cst):
    for p in layers:
        gq, v, og, dec, pin, qkv, gates = _inproj(x, p["wa"], p["wlr"], p["wb"], p["wg"], p["u"], p["ub"],
                                                  in_cst, INPROJ_TILE)
        on = _gla(gq, v, dec, p["ng"], gla_cst)
        yc, yb = _napool(qkv, pin, p["bias"], gla_cst["mkk"], p["pw"], p["ps"], np_cst)
        x = _merge(x, on, og, yb, yc, gates, p["bra"], p["brb"], p["brc"], p["wo"],
                   p["ln1g"], p["ln1b"], DENSE_TILE)
        x = _ffn(x, p["wgate"], p["wup"], p["wdown"], p["ln2g"], p["ln2b"], DENSE_TILE)
    return x


def kernel(x_prompt, x_sample, w_in, gla_up_f, gla_up_b, gla_bias_f, gla_bias_b, gla_norm, pool_w, pool_scale, na_rpb, w_br_a, w_br_b, w_br_c, w_out, ln1_g, ln1_b, w_gate, w_up, w_down, ln2_g, ln2_b):
    weights = (w_in, gla_up_f, gla_up_b, gla_bias_f, gla_bias_b, gla_norm, pool_w, pool_scale, na_rpb,
               w_br_a, w_br_b, w_br_c, w_out, ln1_g, ln1_b, w_gate, w_up, w_down, ln2_g, ln2_b)
    layers = [_prepare_layer(l, *weights) for l in range(w_in.shape[0])]
    in_cst = _inproj_constants()
    gla_cst = _gla_constants()
    np_cst = _napool_constants()
    return (_trunk(x_prompt, layers, in_cst, gla_cst, np_cst),
            _trunk(x_sample, layers, in_cst, gla_cst, np_cst))
```

```python
import functools

import numpy as np
import jax
import jax.numpy as jnp
from jax import lax
from jax.experimental import pallas as pl
from jax.experimental.pallas import tpu as pltpu

f32 = jnp.float32
bf16 = jnp.bfloat16

D_MODEL = 1024
DEPTH = 4
GRID_W = 64
GLA_HEADS = 4
GLA_DK = 64
GLA_DV = 128
GLA_KEY = GLA_HEADS * GLA_DK
GLA_VAL = GLA_HEADS * GLA_DV
GLA_RANK = 16
GLA_TAU = 16.0
GLA_CHUNK = 64
GLA_SUPER = 8
CUMSUM_BLOCK = 256
POOL_GROUPS = 4
POOL_GC = 64
POOL_WIDTH = POOL_GROUPS * POOL_GC
POOL_WINDOWS = (2, 4, 8, 16)
NA_HEADS = 4
NA_HD = 64
NA_WIDTH = NA_HEADS * NA_HD
NA_WIN_R = 8
NA_WIN_C = 16
N_BRANCH = 3
D_FF = 2816
DEEPNORM_ALPHA = (2 * DEPTH) ** 0.25
LN_EPS = 1e-5
RMS_EPS = 1e-6
LR_PAD = 128
NEG_BIG = -1e30
LOG2E = 1.4426950408889634

DENSE_TILE = 1024
ROW_PIECE = 256
INPROJ_TILE = 512

VMEM_LIMIT_BYTES = 56 * 1024 * 1024

_NN = (((1,), (0,)), ((), ()))
_NT = (((1,), (1,)), ((), ()))
_TN = (((0,), (0,)), ((), ()))


def _dot(a, b, dims=_NN):
    return lax.dot_general(a, b, dims, preferred_element_type=f32)


def _const_spec(shape):
    nd = len(shape)
    return pl.BlockSpec(shape, lambda *_: (0,) * nd, pipeline_mode=pl.Buffered(1))


def _params(sem):
    return pltpu.CompilerParams(dimension_semantics=sem, vmem_limit_bytes=VMEM_LIMIT_BYTES)


def _layer_norm(y, g, b):
    mu = jnp.mean(y, axis=-1, keepdims=True)
    yc = y - mu
    var = jnp.mean(yc * yc, axis=-1, keepdims=True)
    return yc * lax.rsqrt(var + LN_EPS) * g + b


def _row_pieces(tm):
    return [pl.ds(r, ROW_PIECE) for r in range(0, tm, ROW_PIECE)]


def _silu(x):
    h = 0.5 * x
    return h * jnp.tanh(h) + h


def _gla_decays(lr, u_ref, ub_ref, tril_ref, triu_ref):
    L = GLA_CHUNK
    tm = lr.shape[0]
    z = _dot(lr.astype(bf16), u_ref[...]) + ub_ref[...]
    la = (jnp.minimum(z, 0.0) - jnp.log(1.0 + jnp.exp(-jnp.abs(z)))) * (1.0 / GLA_TAU)
    hi = la.astype(bf16)
    lo = (la - hi.astype(f32)).astype(bf16)
    out = []
    for d, (tri_ref, last) in enumerate(((tril_ref, L - 1), (triu_ref, 0))):
        cols = slice(d * GLA_KEY, (d + 1) * GLA_KEY)
        tri = tri_ref[...]
        R = tri.shape[0]
        b = jnp.concatenate(
            [_dot(tri, hi[r:r + R, cols]) + _dot(tri, lo[r:r + R, cols]) for r in range(0, tm, R)],
            axis=0)
        ends = b.reshape(tm // L, L, GLA_KEY)[:, last, :]
        out.append((b, ends))
    return out


def _inproj_kernel(x_ref, wa_ref, wlr_ref, wb_ref, wg_ref, u_ref, ub_ref, tril_ref, triu_ref,
                   gq_ref, v_ref, og_ref, dec_ref, p_ref, qkv_ref, g_ref):
    xb = x_ref[...].astype(bf16)
    ra = _dot(xb, wa_ref[...])
    v_ref[...] = ra[:, 2 * GLA_KEY:2 * GLA_KEY + GLA_VAL].astype(bf16)
    og_ref[...] = ra[:, 2 * GLA_KEY + GLA_VAL:].astype(bf16)
    lr = _dot(xb, wlr_ref[...])
    g_ref[...] = _dot(xb, wg_ref[...]).astype(bf16)
    q = ra[:, :GLA_KEY]
    k = ra[:, GLA_KEY:2 * GLA_KEY]
    for d, (b, ends) in enumerate(_gla_decays(lr, u_ref, ub_ref, tril_ref, triu_ref)):
        o = 2 * d * GLA_KEY
        gq_ref[:, o:o + GLA_KEY] = (q * jnp.exp(b)).astype(bf16)
        gq_ref[:, o + GLA_KEY:o + 2 * GLA_KEY] = (k * jnp.exp(-b)).astype(bf16)
        dec_ref[:, d * GLA_KEY:(d + 1) * GLA_KEY] = jnp.exp(ends)
    rb = _dot(xb, wb_ref[...])
    p_ref[...] = rb[:, :POOL_WIDTH].astype(bf16)
    qkv_ref[...] = rb[:, POOL_WIDTH:].astype(bf16)


def _inproj(x, wa, wlr, wb, wg, u, ub, cst, tm):
    B, T, D = x.shape
    assert tm % CUMSUM_BLOCK == 0
    n_chunks = tm // GLA_CHUNK
    widths = (4 * GLA_KEY, GLA_VAL, GLA_VAL, None, POOL_WIDTH, 3 * NA_WIDTH, N_BRANCH * D_MODEL)
    tok = lambda w: pl.BlockSpec((None, tm, w), lambda b, i: (b, i, 0))
    dec_spec = pl.BlockSpec((None, n_chunks, 2 * GLA_KEY), lambda b, i: (b, i, 0))
    consts = (wa, wlr, wb, wg, u, ub, cst["tril"], cst["triu"])
    return pl.pallas_call(
        _inproj_kernel,
        grid=(B, T // tm),
        in_specs=[tok(D)] + [_const_spec(c.shape) for c in consts],
        out_specs=[dec_spec if w is None else tok(w) for w in widths],
        out_shape=[jax.ShapeDtypeStruct((B, T // GLA_CHUNK, 2 * GLA_KEY), f32) if w is None
                   else jax.ShapeDtypeStruct((B, T, w), bf16) for w in widths],
        compiler_params=_params(("parallel", "parallel")),
        name="inproj",
    )(x, *consts)


def _inproj_constants():
    L, n = GLA_CHUNK, CUMSUM_BLOCK // GLA_CHUNK
    i = np.arange(L)
    tril = (i[None, :] <= i[:, None]).astype(np.float32)
    eye = np.eye(n, dtype=np.float32)
    return dict(tril=jnp.asarray(np.kron(eye, tril), bf16), triu=jnp.asarray(np.kron(eye, tril.T), bf16))


def _gla_inter(qc, kic, vc, dec, st, mkk):
    L, H = GLA_CHUNK, GLA_HEADS
    kec = (kic.astype(f32) * dec).astype(bf16)
    q4 = jnp.concatenate([qc] * H, axis=0) * mkk
    oi = _dot(q4, st.astype(bf16), _NT)
    inter = jnp.concatenate([oi[h * L:(h + 1) * L, (h // 2) * GLA_DV:(h // 2 + 1) * GLA_DV] for h in range(H)],
                            axis=1)
    ke4 = jnp.concatenate([kec] * H, axis=0) * mkk
    zero = jnp.zeros((L, GLA_DV), bf16)
    v4 = jnp.concatenate(
        [jnp.concatenate([vc[:, h * GLA_DV:(h + 1) * GLA_DV], zero] if h // 2 == 0 else
                         [zero, vc[:, h * GLA_DV:(h + 1) * GLA_DV]], axis=1) for h in range(H)],
        axis=0)
    return inter, dec * st + _dot(v4, ke4, _TN)


def _gla_super(sc, gq_ref, v_ref, dec_ref, cml_ref, cmu_ref, mkk_ref, mkv_ref, st_ref, fwd):
    L, NS, H = GLA_CHUNK, GLA_SUPER, GLA_HEADS
    R = L * NS
    r0 = pl.multiple_of(sc * R, R)
    o = 0 if fwd else 2 * GLA_KEY
    q_dec = gq_ref[pl.ds(r0, R), o:o + GLA_KEY]
    k_inv = gq_ref[pl.ds(r0, R), o + GLA_KEY:o + 2 * GLA_KEY]
    v = v_ref[pl.ds(r0, R), :]
    if fwd:
        q_oth = gq_ref[pl.ds(r0, R), 2 * GLA_KEY:3 * GLA_KEY]
        k_oth = gq_ref[pl.ds(r0, R), 3 * GLA_KEY:4 * GLA_KEY]
    dcol = 0 if fwd else GLA_KEY
    mkk = mkk_ref[...]
    st = st_ref[...]
    outs = [None] * NS
    for c in (range(NS) if fwd else range(NS - 1, -1, -1)):
        rows = slice(c * L, (c + 1) * L)
        qc, kic, vc = q_dec[rows], k_inv[rows], v[rows]
        dec = dec_ref[pl.ds(sc * NS + c, 1), dcol:dcol + GLA_KEY]
        o_c, st = _gla_inter(qc, kic, vc, dec, st, mkk)
        if fwd:
            kb = jnp.concatenate([kic] * H, axis=0) * mkk
            att_f = _dot(qc, kb, _NT) * cml_ref[...]
            kb = jnp.concatenate([k_oth[rows]] * H, axis=0) * mkk
            att_b = _dot(q_oth[rows], kb, _NT) * cmu_ref[...]
            vb = jnp.concatenate([vc] * H, axis=0) * mkv_ref[...]
            o2 = _dot(jnp.concatenate([att_f, att_b], axis=0).astype(bf16), vb)
            o_c = o_c + (o2[0:L] + o2[L:2 * L])
        outs[c] = o_c
    st_ref[...] = st
    return jnp.concatenate(outs, axis=0)


def _gla_program(gq_ref, v_ref, dec_ref, ng_ref, cml_ref, cmu_ref, mkk_ref, mkv_ref,
                 o_ref, acc_ref, sf_ref, sb_ref, n_super):
    R = GLA_CHUNK * GLA_SUPER
    shared = (mkk_ref, mkv_ref)

    def init():
        sf_ref[...] = jnp.zeros_like(sf_ref)
        sb_ref[...] = jnp.zeros_like(sb_ref)

    def both(i):
        cf, cb = i, n_super - 1 - i
        of = _gla_super(cf, gq_ref, v_ref, dec_ref, cml_ref, cmu_ref, *shared, sf_ref, True)
        ob = _gla_super(cb, gq_ref, v_ref, dec_ref, cml_ref, cmu_ref, *shared, sb_ref, False)
        return cf, cb, of, ob

    def first_half(i):
        cf, cb, of, ob = both(i)
        acc_ref[pl.ds(pl.multiple_of(cf * R, R), R), :] = of
        acc_ref[pl.ds(pl.multiple_of(cb * R, R), R), :] = ob

    def finish(c, o_new):
        r0 = pl.multiple_of(c * R, R)
        o = acc_ref[pl.ds(r0, R), :] + o_new
        outs = []
        for h in range(GLA_HEADS):
            oh = o[:, h * GLA_DV:(h + 1) * GLA_DV]
            ms = jnp.mean(oh * oh, axis=-1, keepdims=True)
            outs.append(oh * lax.rsqrt(ms + RMS_EPS) * ng_ref[...])
        o_ref[pl.ds(r0, R), :] = jnp.concatenate(outs, axis=1).astype(o_ref.dtype)

    def second_half(i):
        cf, cb, of, ob = both(i)
        finish(cf, of)
        finish(cb, ob)

    return init, first_half, second_half


def _gla_constants():
    L, H = GLA_CHUNK, GLA_HEADS
    i = np.arange(L)
    tril = (i[None, :] <= i[:, None]).astype(np.float32)
    hk = np.arange(H * L) // L
    hv = np.arange(H * GLA_DV) // GLA_DV
    mkk = (hk[:, None] == hk[None, :]).astype(np.float32)
    mkv = (hk[:, None] == hv[None, :]).astype(np.float32)
    return dict(
        cml=jnp.asarray(np.tile(tril, (1, H)), f32), cmu=jnp.asarray(np.tile(tril.T, (1, H)), f32),
        mkk=jnp.asarray(mkk, bf16), mkv=jnp.asarray(mkv, bf16))


def _napool_program(qkv_ref, p_ref, bias_ref, mkk_ref, hm_ref, a_ref, half_ref, pw_ref, ps_ref,
                    yc_ref, yb_ref, s0_ref, s1_ref, e0_ref, e1_ref, d0_ref, d1_ref, rows, seq_len):
    W = GRID_W
    wr = NA_WIN_R

    def window(r):
        r0 = jnp.clip(r - wr // 2, 0, rows - wr)
        return r0, pl.multiple_of(r0 * W, W)

    def scores(r, s_ref):
        r = jnp.minimum(r, rows - 1)
        t0 = pl.multiple_of(r * W, W)
        r0, k0 = window(r)
        q = qkv_ref[pl.ds(t0, W), 0:NA_WIDTH]
        kw = qkv_ref[pl.ds(k0, wr * W), NA_WIDTH:2 * NA_WIDTH]
        qs = jnp.concatenate([q] * NA_HEADS, axis=0) * mkk_ref[...]
        s_ref[...] = _dot(qs, kw, _NT) + bias_ref[r - r0]

    def softmax(s_ref, e_ref, d_ref):
        s = s_ref[...]
        e = jnp.exp2(s - jnp.max(s, axis=-1, keepdims=True))
        d_ref[...] = 1.0 / jnp.sum(e, axis=-1, keepdims=True)
        e_ref[...] = e.astype(bf16)

    def finish(r, e_ref, d_ref):
        t0 = pl.multiple_of(r * W, W)
        _, k0 = window(r)
        vw = qkv_ref[pl.ds(k0, wr * W), 2 * NA_WIDTH:3 * NA_WIDTH]
        pv = _dot(e_ref[...], vw) * d_ref[...]
        hm = hm_ref[...]
        out = pv[0:W] * hm[0:1]
        for h in range(1, NA_HEADS):
            out = out + pv[h * W:(h + 1) * W] * hm[h:h + 1]
        yc_ref[pl.ds(t0, W), :] = out.astype(yc_ref.dtype)

    def pool(r):
        t0 = pl.multiple_of(r * W, W)
        blocks = []
        for off in (-1, 0, 1, 2):
            rr = r + off
            blk = p_ref[pl.ds(pl.multiple_of(jnp.clip(rr, 0, rows - 1) * W, W), W), :]
            if off in (-1, 2):
                blk = jnp.where((rr >= 0) & (rr < rows), blk, jnp.zeros_like(blk))
            blocks.append(blk)
        p4 = jnp.concatenate(blocks, axis=0)
        wall = _dot(a_ref[...], p4)
        hm = hm_ref[...]
        half = half_ref[...]
        ds = []
        for i in range(2):
            base = i * POOL_GROUPS * W
            wsum = wall[base:base + W] * hm[0:1]
            for g in range(1, POOL_GROUPS):
                wsum = wsum + wall[base + g * W:base + (g + 1) * W] * hm[g:g + 1]
            t = t0 + i * W + lax.broadcasted_iota(jnp.int32, (W, POOL_WIDTH), 0)
            cnt = (jnp.minimum(t + half, seq_len) - jnp.maximum(t - half, 0)).astype(f32)
            ds.append((wsum / cnt - blocks[1 + i].astype(f32)).astype(bf16))
        yb = _dot(jnp.concatenate(ds, axis=0), pw_ref[...]) * ps_ref[...]
        yb_ref[pl.ds(t0, 2 * W), :] = yb.astype(yb_ref.dtype)

    def prologue():
        scores(0, s0_ref)
        softmax(s0_ref, e0_ref, d0_ref)
        scores(1, s1_ref)

    def step(j):
        r = 2 * j
        finish(r, e0_ref, d0_ref)
        softmax(s1_ref, e1_ref, d1_ref)
        scores(r + 2, s0_ref)
        finish(r + 1, e1_ref, d1_ref)
        softmax(s0_ref, e0_ref, d0_ref)
        scores(r + 3, s1_ref)
        pool(r)

    return prologue, step


def _napool_constants():
    W = GRID_W
    hk = np.arange(NA_WIDTH) // NA_HD
    hm = (np.arange(NA_HEADS)[:, None] == hk[None, :]).astype(np.float32)
    a = np.zeros((2, POOL_GROUPS, W, 4 * W), np.float32)
    t = np.arange(W)[:, None]
    for i in range(2):
        j = np.arange(4 * W)[None, :] - (1 + i) * W
        for g, w in enumerate(POOL_WINDOWS):
            a[i, g] = ((j >= t - w // 2) & (j < t + w - w // 2)).astype(np.float32)
    half = np.repeat(np.array([w // 2 for w in POOL_WINDOWS], np.int32), POOL_GC)[None, :]
    a = a.reshape(2 * POOL_GROUPS * W, 4 * W)
    return dict(hm=jnp.asarray(hm, f32), a=jnp.asarray(a, bf16), half=jnp.asarray(half, jnp.int32))


def _na_bias_table(rpb):
    W, R, C = GRID_W, NA_WIN_R, NA_WIN_C
    cols = np.arange(W)
    c0 = np.clip(cols - C // 2, 0, W - C)
    valid = (cols[None, :] >= c0[:, None]) & (cols[None, :] < c0[:, None] + C)
    row_sel = np.zeros((R, R, 2 * R - 1), np.float32)
    d, kr = np.meshgrid(np.arange(R), np.arange(R), indexing="ij")
    row_sel[d, kr, kr - d + (R - 1)] = 1.0
    col_sel = np.zeros((W, W, 2 * C - 1), np.float32)
    q, kc = np.nonzero(valid)
    col_sel[q, kc, kc - q + (C - 1)] = 1.0
    tab = jnp.einsum("dkr,hrc,qjc->dhqkj", row_sel, rpb * LOG2E, col_sel,
                     precision=lax.Precision.HIGHEST)
    tab = jnp.where(jnp.asarray(valid)[None, None, :, None, :], tab, NEG_BIG)
    return tab.reshape(R, NA_HEADS * W, R * W).astype(f32)


def _step_loop(lo, hi, step):
    def body(i, carry):
        step(i)
        return carry
    lax.fori_loop(lo, hi, body, 0)


def _gla_kernel(*refs, n_super):
    init, first_half, second_half = _gla_program(*refs, n_super)
    init()
    _step_loop(0, n_super // 2, first_half)
    _step_loop(n_super // 2, n_super, second_half)


def _gla(gq, v, dec, ng, cst):
    B, T, _ = gq.shape
    R = GLA_CHUNK * GLA_SUPER
    assert T % (2 * R) == 0
    seq = lambda a: pl.BlockSpec((None,) + a.shape[1:], lambda b: (b, 0, 0))
    consts = (ng, cst["cml"], cst["cmu"], cst["mkk"], cst["mkv"])
    state = pltpu.VMEM((GLA_HEADS // 2 * GLA_DV, GLA_KEY), f32)
    return pl.pallas_call(
        functools.partial(_gla_kernel, n_super=T // R),
        grid=(B,),
        in_specs=[seq(gq), seq(v), seq(dec)] + [_const_spec(c.shape) for c in consts],
        out_specs=pl.BlockSpec((None, T, GLA_VAL), lambda b: (b, 0, 0)),
        out_shape=jax.ShapeDtypeStruct((B, T, GLA_VAL), bf16),
        scratch_shapes=[pltpu.VMEM((T, GLA_VAL), f32), state, state],
        compiler_params=_params(("parallel",)),
        name="gla",
    )(gq, v, dec, *consts)


def _napool_kernel(*refs, rows, seq_len):
    prologue, step = _napool_program(*refs, rows, seq_len)
    prologue()
    _step_loop(0, rows // 2, step)


def _napool(qkv, p, bias, mkk, pw, ps, cst):
    B, T, _ = qkv.shape
    rows = T // GRID_W
    assert rows >= NA_WIN_R and rows % 2 == 0
    seq = lambda w: pl.BlockSpec((None, T, w), lambda b: (b, 0, 0))
    consts = (bias, mkk, cst["hm"], cst["a"], cst["half"], pw, ps)
    sc = (NA_HEADS * GRID_W, NA_WIN_R * GRID_W)
    return pl.pallas_call(
        functools.partial(_napool_kernel, rows=rows, seq_len=T),
        grid=(B,),
        in_specs=[seq(3 * NA_WIDTH), seq(POOL_WIDTH)] + [_const_spec(c.shape) for c in consts],
        out_specs=[seq(NA_WIDTH), seq(POOL_WIDTH)],
        out_shape=[jax.ShapeDtypeStruct((B, T, NA_WIDTH), bf16),
                   jax.ShapeDtypeStruct((B, T, POOL_WIDTH), bf16)],
        scratch_shapes=[pltpu.VMEM(sc, f32), pltpu.VMEM(sc, f32), pltpu.VMEM(sc, bf16), pltpu.VMEM(sc, bf16),
                        pltpu.VMEM((sc[0], 1), f32), pltpu.VMEM((sc[0], 1), f32)],
        compiler_params=_params(("parallel",)),
        name="napool",
    )(qkv, p, *consts)


def _merge_kernel(x_ref, on_ref, og_ref, yb_ref, yc_ref, g_ref, wa_ref, wb_ref, wc_ref, wo_ref,
                  lg_ref, lb_ref, o_ref):
    D = D_MODEL
    one = jnp.ones((), bf16)
    for rows in _row_pieces(x_ref.shape[0]):
        h = og_ref[rows, :]
        a_in = on_ref[rows, :] * (h * jnp.tanh(h) + h)
        m = ((one + jnp.tanh(g_ref[rows, 0:D])) * _dot(a_in, wa_ref[...]).astype(bf16)
             + (one + jnp.tanh(g_ref[rows, D:2 * D])) * _dot(yb_ref[rows, :], wb_ref[...]).astype(bf16)
             + (one + jnp.tanh(g_ref[rows, 2 * D:3 * D])) * _dot(yc_ref[rows, :], wc_ref[...]).astype(bf16))
        y = DEEPNORM_ALPHA * x_ref[rows, :] + _dot(m, wo_ref[...])
        o_ref[rows, :] = _layer_norm(y, lg_ref[...], lb_ref[...])


def _merge(x, on, og, yb, yc, g, wa, wb, wc, wo, lg, lb, tm):
    B, T, D = x.shape
    tok = lambda w: pl.BlockSpec((None, tm, w), lambda b, i: (b, i, 0))
    consts = (wa, wb, wc, wo, lg, lb)
    return pl.pallas_call(
        _merge_kernel,
        grid=(B, T // tm),
        in_specs=[tok(D), tok(GLA_VAL), tok(GLA_VAL), tok(POOL_WIDTH), tok(NA_WIDTH),
                  tok(N_BRANCH * D)] + [_const_spec(c.shape) for c in consts],
        out_specs=tok(D),
        out_shape=jax.ShapeDtypeStruct((B, T, D), f32),
        compiler_params=_params(("parallel", "parallel")),
        name="merge",
    )(x, on, og, yb, yc, g, *consts)


def _ffn_kernel(x_ref, wg_ref, wu_ref, wd_ref, lg_ref, lb_ref, o_ref):
    for rows in _row_pieces(x_ref.shape[0]):
        x = x_ref[rows, :]
        xb = x.astype(bf16)
        hg = _dot(xb, wg_ref[...])
        hu = _dot(xb, wu_ref[...])
        act = (_silu(hg) * hu).astype(bf16)
        y = DEEPNORM_ALPHA * x + _dot(act, wd_ref[...])
        o_ref[rows, :] = _layer_norm(y, lg_ref[...], lb_ref[...])


def _ffn(x, wg, wu, wd, lg, lb, tm):
    B, T, D = x.shape
    tok = pl.BlockSpec((None, tm, D), lambda b, i: (b, i, 0))
    consts = (wg, wu, wd, lg, lb)
    return pl.pallas_call(
        _ffn_kernel,
        grid=(B, T // tm),
        in_specs=[tok] + [_const_spec(c.shape) for c in consts],
        out_specs=tok,
        out_shape=jax.ShapeDtypeStruct((B, T, D), f32),
        compiler_params=_params(("parallel", "parallel")),
        name="ffn",
    )(x, *consts)


def _prepare_layer(l, w_in, gla_up_f, gla_up_b, gla_bias_f, gla_bias_b, gla_norm, pool_w, pool_scale,
                   na_rpb, w_br_a, w_br_b, w_br_c, w_out, ln1_g, ln1_b, w_gate, w_up, w_down,
                   ln2_g, ln2_b):
    wi = w_in[l]
    o_lr = 2 * GLA_KEY + 2 * GLA_VAL
    o_p = o_lr + 2 * GLA_RANK
    o_g = o_p + POOL_WIDTH + 3 * NA_WIDTH
    wlr = jnp.zeros((D_MODEL, LR_PAD), f32).at[:, :2 * GLA_RANK].set(wi[:, o_lr:o_p])
    u = jnp.zeros((LR_PAD, 2 * GLA_KEY), f32)
    u = u.at[:GLA_RANK, :GLA_KEY].set(gla_up_f[l]).at[GLA_RANK:2 * GLA_RANK, GLA_KEY:].set(gla_up_b[l])
    row = lambda a: a.reshape(1, -1).astype(f32)
    o_og = 2 * GLA_KEY + GLA_VAL
    o_cq = o_p + POOL_WIDTH
    wb = jnp.concatenate([wi[:, o_p:o_cq], (NA_HD ** -0.5 * LOG2E) * wi[:, o_cq:o_cq + NA_WIDTH],
                          wi[:, o_cq + NA_WIDTH:o_g]], axis=1)
    wa = jnp.concatenate([(GLA_DK ** -0.5) * wi[:, :GLA_KEY], wi[:, GLA_KEY:o_og], 0.5 * wi[:, o_og:o_lr]], axis=1)
    return dict(
        wa=wa.astype(bf16), wlr=wlr.astype(bf16), wb=wb.astype(bf16),
        wg=(0.5 * wi[:, o_g:]).astype(bf16),
        u=u.astype(bf16), ub=row(jnp.concatenate([gla_bias_f[l], gla_bias_b[l]])),
        ng=row(gla_norm[l]),
        pw=jax.scipy.linalg.block_diag(*[pool_w[l, g] for g in range(POOL_GROUPS)]).astype(bf16),
        ps=row(pool_scale[l]), bias=_na_bias_table(na_rpb[l]),
        bra=(0.5 * w_br_a[l]).astype(bf16), brb=(0.5 * w_br_b[l]).astype(bf16),
        brc=(0.5 * w_br_c[l]).astype(bf16),
        wo=w_out[l].astype(bf16), ln1g=row(ln1_g[l]), ln1b=row(ln1_b[l]),
        wgate=w_gate[l].astype(bf16), wup=w_up[l].astype(bf16), wdown=w_down[l].astype(bf16),
        ln2g=row(ln2_g[l]), ln2b=row(ln2_b[l]))


def _trunk(x, layers, in_cst, gla_cst, np_cst):
    for p in layers:
        gq, v, og, dec, pin, qkv, gates = _inproj(x, p["wa"], p["wlr"], p["wb"], p["wg"], p["u"], p["ub"],
                                                  in_cst, INPROJ_TILE)
        on = _gla(gq, v, dec, p["ng"], gla_cst)
        yc, yb = _napool(qkv, pin, p["bias"], gla_cst["mkk"], p["pw"], p["ps"], np_cst)
        x = _merge(x, on, og, yb, yc, gates, p["bra"], p["brb"], p["brc"], p["wo"],
                   p["ln1g"], p["ln1b"], DENSE_TILE)
        x = _ffn(x, p["wgate"], p["wup"], p["wdown"], p["ln2g"], p["ln2b"], DENSE_TILE)
    return x


def kernel(x_prompt, x_sample, w_in, gla_up_f, gla_up_b, gla_bias_f, gla_bias_b, gla_norm, pool_w, pool_scale, na_rpb, w_br_a, w_br_b, w_br_c, w_out, ln1_g, ln1_b, w_gate, w_up, w_down, ln2_g, ln2_b):
    weights = (w_in, gla_up_f, gla_up_b, gla_bias_f, gla_bias_b, gla_norm, pool_w, pool_scale, na_rpb,
               w_br_a, w_br_b, w_br_c, w_out, ln1_g, ln1_b, w_gate, w_up, w_down, ln2_g, ln2_b)
    layers = [_prepare_layer(l, *weights) for l in range(w_in.shape[0])]
    in_cst = _inproj_constants()
    gla_cst = _gla_constants()
    np_cst = _napool_constants()
    return (_trunk(x_prompt, layers, in_cst, gla_cst, np_cst),
            _trunk(x_sample, layers, in_cst, gla_cst, np_cst))
```

```python
import functools

import numpy as np
import jax
import jax.numpy as jnp
from jax import lax
from jax.experimental import pallas as pl
from jax.experimental.pallas import tpu as pltpu

f32 = jnp.float32
bf16 = jnp.bfloat16

D_MODEL = 1024
DEPTH = 4
GRID_W = 64
GLA_HEADS = 4
GLA_DK = 64
GLA_DV = 128
GLA_KEY = GLA_HEADS * GLA_DK
GLA_VAL = GLA_HEADS * GLA_DV
GLA_RANK = 16
GLA_TAU = 16.0
GLA_CHUNK = 64
GLA_SUPER = 8
CUMSUM_BLOCK = 256
POOL_GROUPS = 4
POOL_GC = 64
POOL_WIDTH = POOL_GROUPS * POOL_GC
POOL_WINDOWS = (2, 4, 8, 16)
NA_HEADS = 4
NA_HD = 64
NA_WIDTH = NA_HEADS * NA_HD
NA_WIN_R = 8
NA_WIN_C = 16
N_BRANCH = 3
DEEPNORM_ALPHA = (2 * DEPTH) ** 0.25
LN_EPS = 1e-5
RMS_EPS = 1e-6
LR_PAD = 128
NEG_BIG = -1e30
LOG2E = 1.4426950408889634

DENSE_TILE = 1024
ROW_PIECE = 256
INPROJ_TILE = 512

VMEM_LIMIT_BYTES = 56 * 1024 * 1024

_NN = (((1,), (0,)), ((), ()))
_NT = (((1,), (1,)), ((), ()))
_TN = (((0,), (0,)), ((), ()))


def _dot(a, b, dims=_NN):
    return lax.dot_general(a, b, dims, preferred_element_type=f32)


def _const_spec(shape):
    nd = len(shape)
    return pl.BlockSpec(shape, lambda *_: (0,) * nd, pipeline_mode=pl.Buffered(1))


def _params(sem):
    return pltpu.CompilerParams(dimension_semantics=sem, vmem_limit_bytes=VMEM_LIMIT_BYTES)


def _layer_norm(y, g, b):
    mu = jnp.mean(y, axis=-1, keepdims=True)
    yc = y - mu
    var = jnp.mean(yc * yc, axis=-1, keepdims=True)
    return yc * lax.rsqrt(var + LN_EPS) * g + b


def _row_pieces(tm):
    return [pl.ds(r, ROW_PIECE) for r in range(0, tm, ROW_PIECE)]


def _silu(x):
    h = 0.5 * x
    return h * jnp.tanh(h) + h


def _gla_decays(lr, u_ref, ub_ref, tril_ref, triu_ref):
    L = GLA_CHUNK
    tm = lr.shape[0]
    z = _dot(lr.astype(bf16), u_ref[...]) + ub_ref[...]
    la = (jnp.minimum(z, 0.0) - jnp.log(1.0 + jnp.exp(-jnp.abs(z)))) * (1.0 / GLA_TAU)
    hi = la.astype(bf16)
    lo = (la - hi.astype(f32)).astype(bf16)
    out = []
    for d, (tri_ref, last) in enumerate(((tril_ref, L - 1), (triu_ref, 0))):
        cols = slice(d * GLA_KEY, (d + 1) * GLA_KEY)
        tri = tri_ref[...]
        R = tri.shape[0]
        b = jnp.concatenate(
            [_dot(tri, hi[r:r + R, cols]) + _dot(tri, lo[r:r + R, cols]) for r in range(0, tm, R)],
            axis=0)
        ends = b.reshape(tm // L, L, GLA_KEY)[:, last, :]
        out.append((b, ends))
    return out


def _inproj_kernel(x_ref, wa_ref, wlr_ref, wb_ref, wg_ref, u_ref, ub_ref, tril_ref, triu_ref,
                   gq_ref, v_ref, og_ref, dec_ref, p_ref, qkv_ref, g_ref):
    xb = x_ref[...].astype(bf16)
    ra = _dot(xb, wa_ref[...])
    v_ref[...] = ra[:, 2 * GLA_KEY:2 * GLA_KEY + GLA_VAL].astype(bf16)
    og_ref[...] = ra[:, 2 * GLA_KEY + GLA_VAL:].astype(bf16)
    lr = _dot(xb, wlr_ref[...])
    g_ref[...] = _dot(xb, wg_ref[...]).astype(bf16)
    q = ra[:, :GLA_KEY]
    k = ra[:, GLA_KEY:2 * GLA_KEY]
    for d, (b, ends) in enumerate(_gla_decays(lr, u_ref, ub_ref, tril_ref, triu_ref)):
        o = 2 * d * GLA_KEY
        gq_ref[:, o:o + GLA_KEY] = (q * jnp.exp(b)).astype(bf16)
        gq_ref[:, o + GLA_KEY:o + 2 * GLA_KEY] = (k * jnp.exp(-b)).astype(bf16)
        dec_ref[:, d * GLA_KEY:(d + 1) * GLA_KEY] = jnp.exp(ends)
    rb = _dot(xb, wb_ref[...])
    p_ref[...] = rb[:, :POOL_WIDTH].astype(bf16)
    qkv_ref[...] = rb[:, POOL_WIDTH:].astype(bf16)


def _inproj(x, wa, wlr, wb, wg, u, ub, cst, tm):
    B, T, D = x.shape
    assert tm % CUMSUM_BLOCK == 0
    n_chunks = tm // GLA_CHUNK
    widths = (4 * GLA_KEY, GLA_VAL, GLA_VAL, None, POOL_WIDTH, 3 * NA_WIDTH, N_BRANCH * D_MODEL)
    tok = lambda w: pl.BlockSpec((None, tm, w), lambda b, i: (b, i, 0))
    dec_spec = pl.BlockSpec((None, n_chunks, 2 * GLA_KEY), lambda b, i: (b, i, 0))
    consts = (wa, wlr, wb, wg, u, ub, cst["tril"], cst["triu"])
    return pl.pallas_call(
        _inproj_kernel,
        grid=(B, T // tm),
        in_specs=[tok(D)] + [_const_spec(c.shape) for c in consts],
        out_specs=[dec_spec if w is None else tok(w) for w in widths],
        out_shape=[jax.ShapeDtypeStruct((B, T // GLA_CHUNK, 2 * GLA_KEY), f32) if w is None
                   else jax.ShapeDtypeStruct((B, T, w), bf16) for w in widths],
        compiler_params=_params(("parallel", "parallel")),
        name="inproj",
    )(x, *consts)


def _inproj_constants():
    L, n = GLA_CHUNK, CUMSUM_BLOCK // GLA_CHUNK
    i = np.arange(L)
    tril = (i[None, :] <= i[:, None]).astype(np.float32)
    eye = np.eye(n, dtype=np.float32)
    return dict(tril=jnp.asarray(np.kron(eye, tril), bf16), triu=jnp.asarray(np.kron(eye, tril.T), bf16))


def _gla_inter(qc, kic, vc, dec, st, mkk):
    L, H = GLA_CHUNK, GLA_HEADS
    kec = (kic.astype(f32) * dec).astype(bf16)
    q4 = jnp.concatenate([qc] * H, axis=0) * mkk
    oi = _dot(q4, st.astype(bf16), _NT)
    inter = jnp.concatenate([oi[h * L:(h + 1) * L, (h // 2) * GLA_DV:(h // 2 + 1) * GLA_DV] for h in range(H)],
                            axis=1)
    ke4 = jnp.concatenate([kec] * H, axis=0) * mkk
    zero = jnp.zeros((L, GLA_DV), bf16)
    v4 = jnp.concatenate(
        [jnp.concatenate([vc[:, h * GLA_DV:(h + 1) * GLA_DV], zero] if h // 2 == 0 else
                         [zero, vc[:, h * GLA_DV:(h + 1) * GLA_DV]], axis=1) for h in range(H)],
        axis=0)
    return inter, dec * st + _dot(v4, ke4, _TN)


def _gla_super(sc, gq_ref, v_ref, dec_ref, cml_ref, cmu_ref, mkk_ref, mkv_ref, st_ref, fwd):
    L, NS, H = GLA_CHUNK, GLA_SUPER, GLA_HEADS
    R = L * NS
    r0 = pl.multiple_of(sc * R, R)
    o = 0 if fwd else 2 * GLA_KEY
    q_dec = gq_ref[pl.ds(r0, R), o:o + GLA_KEY]
    k_inv = gq_ref[pl.ds(r0, R), o + GLA_KEY:o + 2 * GLA_KEY]
    v = v_ref[pl.ds(r0, R), :]
    if fwd:
        q_oth = gq_ref[pl.ds(r0, R), 2 * GLA_KEY:3 * GLA_KEY]
        k_oth = gq_ref[pl.ds(r0, R), 3 * GLA_KEY:4 * GLA_KEY]
    dcol = 0 if fwd else GLA_KEY
    mkk = mkk_ref[...]
    st = st_ref[...]
    outs = [None] * NS
    for c in (range(NS) if fwd else range(NS - 1, -1, -1)):
        rows = slice(c * L, (c + 1) * L)
        qc, kic, vc = q_dec[rows], k_inv[rows], v[rows]
        dec = dec_ref[pl.ds(sc * NS + c, 1), dcol:dcol + GLA_KEY]
        o_c, st = _gla_inter(qc, kic, vc, dec, st, mkk)
        if fwd:
            kb = jnp.concatenate([kic] * H, axis=0) * mkk
            att_f = _dot(qc, kb, _NT) * cml_ref[...]
            kb = jnp.concatenate([k_oth[rows]] * H, axis=0) * mkk
            att_b = _dot(q_oth[rows], kb, _NT) * cmu_ref[...]
            vb = jnp.concatenate([vc] * H, axis=0) * mkv_ref[...]
            o2 = _dot(jnp.concatenate([att_f, att_b], axis=0).astype(bf16), vb)
            o_c = o_c + (o2[0:L] + o2[L:2 * L])
        outs[c] = o_c
    st_ref[...] = st
    return jnp.concatenate(outs, axis=0)


def _gla_program(gq_ref, v_ref, dec_ref, ng_ref, cml_ref, cmu_ref, mkk_ref, mkv_ref,
                 o_ref, acc_ref, sf_ref, sb_ref, n_super):
    R = GLA_CHUNK * GLA_SUPER
    shared = (mkk_ref, mkv_ref)

    def init():
        sf_ref[...] = jnp.zeros_like(sf_ref)
        sb_ref[...] = jnp.zeros_like(sb_ref)

    def both(i):
        cf, cb = i, n_super - 1 - i
        of = _gla_super(cf, gq_ref, v_ref, dec_ref, cml_ref, cmu_ref, *shared, sf_ref, True)
        ob = _gla_super(cb, gq_ref, v_ref, dec_ref, cml_ref, cmu_ref, *shared, sb_ref, False)
        return cf, cb, of, ob

    def first_half(i):
        cf, cb, of, ob = both(i)
        acc_ref[pl.ds(pl.multiple_of(cf * R, R), R), :] = of
        acc_ref[pl.ds(pl.multiple_of(cb * R, R), R), :] = ob

    def finish(c, o_new):
        r0 = pl.multiple_of(c * R, R)
        o = acc_ref[pl.ds(r0, R), :] + o_new
        outs = []
        for h in range(GLA_HEADS):
            oh = o[:, h * GLA_DV:(h + 1) * GLA_DV]
            ms = jnp.mean(oh * oh, axis=-1, keepdims=True)
            outs.append(oh * lax.rsqrt(ms + RMS_EPS) * ng_ref[...])
        o_ref[pl.ds(r0, R), :] = jnp.concatenate(outs, axis=1).astype(o_ref.dtype)

    def second_half(i):
        cf, cb, of, ob = both(i)
        finish(cf, of)
        finish(cb, ob)

    return init, first_half, second_half


def _gla_constants():
    L, H = GLA_CHUNK, GLA_HEADS
    i = np.arange(L)
    tril = (i[None, :] <= i[:, None]).astype(np.float32)
    hk = np.arange(H * L) // L
    hv = np.arange(H * GLA_DV) // GLA_DV
    mkk = (hk[:, None] == hk[None, :]).astype(np.float32)
    mkv = (hk[:, None] == hv[None, :]).astype(np.float32)
    return dict(
        cml=jnp.asarray(np.tile(tril, (1, H)), f32), cmu=jnp.asarray(np.tile(tril.T, (1, H)), f32),
        mkk=jnp.asarray(mkk, bf16), mkv=jnp.asarray(mkv, bf16))


def _napool_program(qkv_ref, p_ref, bias_ref, mkk_ref, hm_ref, a_ref, half_ref, pw_ref, ps_ref,
                    yc_ref, yb_ref, s0_ref, s1_ref, e0_ref, e1_ref, d0_ref, d1_ref, rows, seq_len):
    W = GRID_W
    wr = NA_WIN_R

    def window(r):
        r0 = jnp.clip(r - wr // 2, 0, rows - wr)
        return r0, pl.multiple_of(r0 * W, W)

    def scores(r, s_ref):
        r = jnp.minimum(r, rows - 1)
        t0 = pl.multiple_of(r * W, W)
        r0, k0 = window(r)
        q = qkv_ref[pl.ds(t0, W), 0:NA_WIDTH]
        kw = qkv_ref[pl.ds(k0, wr * W), NA_WIDTH:2 * NA_WIDTH]
        qs = jnp.concatenate([q] * NA_HEADS, axis=0) * mkk_ref[...]
        s_ref[...] = _dot(qs, kw, _NT) + bias_ref[r - r0]

    def softmax(s_ref, e_ref, d_ref):
        s = s_ref[...]
        e = jnp.exp2(s - jnp.max(s, axis=-1, keepdims=True))
        d_ref[...] = 1.0 / jnp.sum(e, axis=-1, keepdims=True)
        e_ref[...] = e.astype(bf16)

    def finish(r, e_ref, d_ref):
        t0 = pl.multiple_of(r * W, W)
        _, k0 = window(r)
        vw = qkv_ref[pl.ds(k0, wr * W), 2 * NA_WIDTH:3 * NA_WIDTH]
        pv = _dot(e_ref[...], vw) * d_ref[...]
        hm = hm_ref[...]
        out = pv[0:W] * hm[0:1]
        for h in range(1, NA_HEADS):
            out = out + pv[h * W:(h + 1) * W] * hm[h:h + 1]
        yc_ref[pl.ds(t0, W), :] = out.astype(yc_ref.dtype)

    def pool(r):
        t0 = pl.multiple_of(r * W, W)
        blocks = []
        for off in (-1, 0, 1, 2):
            rr = r + off
            blk = p_ref[pl.ds(pl.multiple_of(jnp.clip(rr, 0, rows - 1) * W, W), W), :]
            if off in (-1, 2):
                blk = jnp.where((rr >= 0) & (rr < rows), blk, jnp.zeros_like(blk))
            blocks.append(blk)
        p4 = jnp.concatenate(blocks, axis=0)
        wall = _dot(a_ref[...], p4)
        hm = hm_ref[...]
        half = half_ref[...]
        ds = []
        for i in range(2):
            base = i * POOL_GROUPS * W
            wsum = wall[base:base + W] * hm[0:1]
            for g in range(1, POOL_GROUPS):
                wsum = wsum + wall[base + g * W:base + (g + 1) * W] * hm[g:g + 1]
            t = t0 + i * W + lax.broadcasted_iota(jnp.int32, (W, POOL_WIDTH), 0)
            cnt = (jnp.minimum(t + half, seq_len) - jnp.maximum(t - half, 0)).astype(f32)
            ds.append((wsum / cnt - blocks[1 + i].astype(f32)).astype(bf16))
        yb = _dot(jnp.concatenate(ds, axis=0), pw_ref[...]) * ps_ref[...]
        yb_ref[pl.ds(t0, 2 * W), :] = yb.astype(yb_ref.dtype)

    def prologue():
        scores(0, s0_ref)
        softmax(s0_ref, e0_ref, d0_ref)
        scores(1, s1_ref)

    def step(j):
        r = 2 * j
        finish(r, e0_ref, d0_ref)
        softmax(s1_ref, e1_ref, d1_ref)
        scores(r + 2, s0_ref)
        finish(r + 1, e1_ref, d1_ref)
        softmax(s0_ref, e0_ref, d0_ref)
        scores(r + 3, s1_ref)
        pool(r)

    return prologue, step


def _napool_constants():
    W = GRID_W
    hk = np.arange(NA_WIDTH) // NA_HD
    hm = (np.arange(NA_HEADS)[:, None] == hk[None, :]).astype(np.float32)
    a = np.zeros((2, POOL_GROUPS, W, 4 * W), np.float32)
    t = np.arange(W)[:, None]
    for i in range(2):
        j = np.arange(4 * W)[None, :] - (1 + i) * W
        for g, w in enumerate(POOL_WINDOWS):
            a[i, g] = ((j >= t - w // 2) & (j < t + w - w // 2)).astype(np.float32)
    half = np.repeat(np.array([w // 2 for w in POOL_WINDOWS], np.int32), POOL_GC)[None, :]
    a = a.reshape(2 * POOL_GROUPS * W, 4 * W)
    return dict(hm=jnp.asarray(hm, f32), a=jnp.asarray(a, bf16), half=jnp.asarray(half, jnp.int32))


def _na_bias_table(rpb):
    W, R, C = GRID_W, NA_WIN_R, NA_WIN_C
    cols = np.arange(W)
    c0 = np.clip(cols - C // 2, 0, W - C)
    valid = (cols[None, :] >= c0[:, None]) & (cols[None, :] < c0[:, None] + C)
    row_sel = np.zeros((R, R, 2 * R - 1), np.float32)
    d, kr = np.meshgrid(np.arange(R), np.arange(R), indexing="ij")
    row_sel[d, kr, kr - d + (R - 1)] = 1.0
    col_sel = np.zeros((W, W, 2 * C - 1), np.float32)
    q, kc = np.nonzero(valid)
    col_sel[q, kc, kc - q + (C - 1)] = 1.0
    tab = jnp.einsum("dkr,hrc,qjc->dhqkj", row_sel, rpb * LOG2E, col_sel,
                     precision=lax.Precision.HIGHEST)
    tab = jnp.where(jnp.asarray(valid)[None, None, :, None, :], tab, NEG_BIG)
    return tab.reshape(R, NA_HEADS * W, R * W).astype(f32)


def _step_loop(lo, hi, step):
    def body(i, carry):
        step(i)
        return carry
    lax.fori_loop(lo, hi, body, 0)


def _gla_kernel(*refs, n_super):
    init, first_half, second_half = _gla_program(*refs, n_super)
    init()
    _step_loop(0, n_super // 2, first_half)
    _step_loop(n_super // 2, n_super, second_half)


def _gla(gq, v, dec, ng, cst):
    B, T, _ = gq.shape
    R = GLA_CHUNK * GLA_SUPER
    assert T % (2 * R) == 0
    seq = lambda a: pl.BlockSpec((None,) + a.shape[1:], lambda b: (b, 0, 0))
    consts = (ng, cst["cml"], cst["cmu"], cst["mkk"], cst["mkv"])
    state = pltpu.VMEM((GLA_HEADS // 2 * GLA_DV, GLA_KEY), f32)
    return pl.pallas_call(
        functools.partial(_gla_kernel, n_super=T // R),
        grid=(B,),
        in_specs=[seq(gq), seq(v), seq(dec)] + [_const_spec(c.shape) for c in consts],
        out_specs=pl.BlockSpec((None, T, GLA_VAL), lambda b: (b, 0, 0)),
        out_shape=jax.ShapeDtypeStruct((B, T, GLA_VAL), bf16),
        scratch_shapes=[pltpu.VMEM((T, GLA_VAL), f32), state, state],
        compiler_params=_params(("parallel",)),
        name="gla",
    )(gq, v, dec, *consts)


def _napool_kernel(*refs, rows, seq_len):
    prologue, step = _napool_program(*refs, rows, seq_len)
    prologue()
    _step_loop(0, rows // 2, step)


def _napool(qkv, p, bias, mkk, pw, ps, cst):
    B, T, _ = qkv.shape
    rows = T // GRID_W
    assert rows >= NA_WIN_R and rows % 2 == 0
    seq = lambda w: pl.BlockSpec((None, T, w), lambda b: (b, 0, 0))
    consts = (bias, mkk, cst["hm"], cst["a"], cst["half"], pw, ps)
    sc = (NA_HEADS * GRID_W, NA_WIN_R * GRID_W)
    return pl.pallas_call(
        functools.partial(_napool_kernel, rows=rows, seq_len=T),
        grid=(B,),
        in_specs=[seq(3 * NA_WIDTH), seq(POOL_WIDTH)] + [_const_spec(c.shape) for c in consts],
        out_specs=[seq(NA_WIDTH), seq(POOL_WIDTH)],
        out_shape=[jax.ShapeDtypeStruct((B, T, NA_WIDTH), bf16),
                   jax.ShapeDtypeStruct((B, T, POOL_WIDTH), bf16)],
        scratch_shapes=[pltpu.VMEM(sc, f32), pltpu.VMEM(sc, f32), pltpu.VMEM(sc, bf16), pltpu.VMEM(sc, bf16),
                        pltpu.VMEM((sc[0], 1), f32), pltpu.VMEM((sc[0], 1), f32)],
        compiler_params=_params(("parallel",)),
        name="napool",
    )(qkv, p, *consts)


def _merge_kernel(x_ref, on_ref, og_ref, yb_ref, yc_ref, g_ref, wa_ref, wb_ref, wc_ref, wo_ref,
                  lg_ref, lb_ref, o_ref):
    D = D_MODEL
    one = jnp.ones((), bf16)
    for rows in _row_pieces(x_ref.shape[0]):
        h = og_ref[rows, :]
        a_in = on_ref[rows, :] * (h * jnp.tanh(h) + h)
        m = ((one + jnp.tanh(g_ref[rows, 0:D])) * _dot(a_in, wa_ref[...]).astype(bf16)
             + (one + jnp.tanh(g_ref[rows, D:2 * D])) * _dot(yb_ref[rows, :], wb_ref[...]).astype(bf16)
             + (one + jnp.tanh(g_ref[rows, 2 * D:3 * D])) * _dot(yc_ref[rows, :], wc_ref[...]).astype(bf16))
        y = DEEPNORM_ALPHA * x_ref[rows, :] + _dot(m, wo_ref[...])
        o_ref[rows, :] = _layer_norm(y, lg_ref[...], lb_ref[...])


def _merge(x, on, og, yb, yc, g, wa, wb, wc, wo, lg, lb, tm):
    B, T, D = x.shape
    tok = lambda w: pl.BlockSpec((None, tm, w), lambda b, i: (b, i, 0))
    consts = (wa, wb, wc, wo, lg, lb)
    return pl.pallas_call(
        _merge_kernel,
        grid=(B, T // tm),
        in_specs=[tok(D), tok(GLA_VAL), tok(GLA_VAL), tok(POOL_WIDTH), tok(NA_WIDTH),
                  tok(N_BRANCH * D)] + [_const_spec(c.shape) for c in consts],
        out_specs=tok(D),
        out_shape=jax.ShapeDtypeStruct((B, T, D), f32),
        compiler_params=_params(("parallel", "parallel")),
        name="merge",
    )(x, on, og, yb, yc, g, *consts)


def _ffn_kernel(x_ref, wg_ref, wu_ref, wd_ref, lg_ref, lb_ref, o_ref):
    for rows in _row_pieces(x_ref.shape[0]):
        x = x_ref[rows, :]
        xb = x.astype(bf16)
        hg = _dot(xb, wg_ref[...])
        hu = _dot(xb, wu_ref[...])
        act = (_silu(hg) * hu).astype(bf16)
        y = DEEPNORM_ALPHA * x + _dot(act, wd_ref[...])
        o_ref[rows, :] = _layer_norm(y, lg_ref[...], lb_ref[...])


def _ffn(x, wg, wu, wd, lg, lb, tm):
    B, T, D = x.shape
    tok = pl.BlockSpec((None, tm, D), lambda b, i: (b, i, 0))
    consts = (wg, wu, wd, lg, lb)
    return pl.pallas_call(
        _ffn_kernel,
        grid=(B, T // tm),
        in_specs=[tok] + [_const_spec(c.shape) for c in consts],
        out_specs=tok,
        out_shape=jax.ShapeDtypeStruct((B, T, D), f32),
        compiler_params=_params(("parallel", "parallel")),
        name="ffn",
    )(x, *consts)


def _prepare_layer(l, w_in, gla_up_f, gla_up_b, gla_bias_f, gla_bias_b, gla_norm, pool_w, pool_scale,
                   na_rpb, w_br_a, w_br_b, w_br_c, w_out, ln1_g, ln1_b, w_gate, w_up, w_down,
                   ln2_g, ln2_b):
    wi = w_in[l]
    o_lr = 2 * GLA_KEY + 2 * GLA_VAL
    o_p = o_lr + 2 * GLA_RANK
    o_g = o_p + POOL_WIDTH + 3 * NA_WIDTH
    wlr = jnp.zeros((D_MODEL, LR_PAD), f32).at[:, :2 * GLA_RANK].set(wi[:, o_lr:o_p])
    u = jnp.zeros((LR_PAD, 2 * GLA_KEY), f32)
    u = u.at[:GLA_RANK, :GLA_KEY].set(gla_up_f[l]).at[GLA_RANK:2 * GLA_RANK, GLA_KEY:].set(gla_up_b[l])
    row = lambda a: a.reshape(1, -1).astype(f32)
    o_og = 2 * GLA_KEY + GLA_VAL
    o_cq = o_p + POOL_WIDTH
    wb = jnp.concatenate([wi[:, o_p:o_cq], (NA_HD ** -0.5 * LOG2E) * wi[:, o_cq:o_cq + NA_WIDTH],
                          wi[:, o_cq + NA_WIDTH:o_g]], axis=1)
    wa = jnp.concatenate([(GLA_DK ** -0.5) * wi[:, :GLA_KEY], wi[:, GLA_KEY:o_og], 0.5 * wi[:, o_og:o_lr]], axis=1)
    return dict(
        wa=wa.astype(bf16), wlr=wlr.astype(bf16), wb=wb.astype(bf16),
        wg=(0.5 * wi[:, o_g:]).astype(bf16),
        u=u.astype(bf16), ub=row(jnp.concatenate([gla_bias_f[l], gla_bias_b[l]])),
        ng=row(gla_norm[l]),
        pw=jax.scipy.linalg.block_diag(*[pool_w[l, g] for g in range(POOL_GROUPS)]).astype(bf16),
        ps=row(pool_scale[l]), bias=_na_bias_table(na_rpb[l]),
        bra=(0.5 * w_br_a[l]).astype(bf16), brb=(0.5 * w_br_b[l]).astype(bf16),
        brc=(0.5 * w_br_c[l]).astype(bf16),
        wo=w_out[l].astype(bf16), ln1g=row(ln1_g[l]), ln1b=row(ln1_b[l]),
        wgate=w_gate[l].astype(bf16), wup=w_up[l].astype(bf16), wdown=w_down[l].astype(bf16),
        ln2g=row(ln2_g[l]), ln2b=row(ln2_b[l]))


def _trunk(x, layers, in_cst, gla_cst, np_cst):
    for p in layers:
        gq, v, og, dec, pin, qkv, gates = _inproj(x, p["wa"], p["wlr"], p["wb"], p["wg"], p["u"], p["ub"],
                                                  in_cst, INPROJ_TILE)
        on = _gla(gq, v, dec, p["ng"], gla_cst)
        yc, yb = _napool(qkv, pin, p["bias"], gla_cst["mkk"], p["pw"], p["ps"], np_cst)
        x = _merge(x, on, og, yb, yc, gates, p["bra"], p["brb"], p["brc"], p["wo"],
                   p["ln1g"], p["ln1b"], DENSE_TILE)
        x = _ffn(x, p["wgate"], p["wup"], p["wdown"], p["ln2g"], p["ln2b"], DENSE_TILE)
    return x


def kernel(x_prompt, x_sample, w_in, gla_up_f, gla_up_b, gla_bias_f, gla_bias_b, gla_norm, pool_w, pool_scale, na_rpb, w_br_a, w_br_b, w_br_c, w_out, ln1_g, ln1_b, w_gate, w_up, w_down, ln2_g, ln2_b):
    weights = (w_in, gla_up_f, gla_up_b, gla_bias_f, gla_bias_b, gla_norm, pool_w, pool_scale, na_rpb,
               w_br_a, w_br_b, w_br_c, w_out, ln1_g, ln1_b, w_gate, w_up, w_down, ln2_g, ln2_b)
    layers = [_prepare_layer(l, *weights) for l in range(w_in.shape[0])]
    in_cst = _inproj_constants()
    gla_cst = _gla_constants()
    np_cst = _napool_constants()
    return (_trunk(x_prompt, layers, in_cst, gla_cst, np_cst),
            _trunk(x_sample, layers, in_cst, gla_cst, np_cst))
```

```python
import functools

import numpy as np
import jax
import jax.numpy as jnp
from jax import lax
from jax.experimental import pallas as pl
from jax.experimental.pallas import tpu as pltpu

f32 = jnp.float32
bf16 = jnp.bfloat16

D_MODEL = 1024
DEPTH = 4
GRID_W = 64
GLA_HEADS = 4
GLA_DK = 64
GLA_DV = 128
GLA_KEY = GLA_HEADS * GLA_DK
GLA_VAL = GLA_HEADS * GLA_DV
GLA_RANK = 16
GLA_TAU = 16.0
GLA_CHUNK = 64
GLA_SUPER = 8
CUMSUM_BLOCK = 256
POOL_GROUPS = 4
POOL_GC = 64
POOL_WIDTH = POOL_GROUPS * POOL_GC
POOL_WINDOWS = (2, 4, 8, 16)
NA_HEADS = 4
NA_HD = 64
NA_WIDTH = NA_HEADS * NA_HD
NA_WIN_R = 8
NA_WIN_C = 16
N_BRANCH = 3
DEEPNORM_ALPHA = (2 * DEPTH) ** 0.25
LN_EPS = 1e-5
RMS_EPS = 1e-6
LR_PAD = 128
NEG_BIG = -1e30
LOG2E = 1.4426950408889634

DENSE_TILE = 1024
ROW_PIECE = 256
INPROJ_TILE = 512

VMEM_LIMIT_BYTES = 56 * 1024 * 1024

_NN = (((1,), (0,)), ((), ()))
_NT = (((1,), (1,)), ((), ()))
_TN = (((0,), (0,)), ((), ()))


def _dot(a, b, dims=_NN):
    return lax.dot_general(a, b, dims, preferred_element_type=f32)


def _const_spec(shape):
    nd = len(shape)
    return pl.BlockSpec(shape, lambda *_: (0,) * nd, pipeline_mode=pl.Buffered(1))


def _params(sem):
    return pltpu.CompilerParams(dimension_semantics=sem, vmem_limit_bytes=VMEM_LIMIT_BYTES)


def _layer_norm(y, g, b):
    mu = jnp.mean(y, axis=-1, keepdims=True)
    yc = y - mu
    var = jnp.mean(yc * yc, axis=-1, keepdims=True)
    return yc * lax.rsqrt(var + LN_EPS) * g + b


def _row_pieces(tm):
    return [pl.ds(r, ROW_PIECE) for r in range(0, tm, ROW_PIECE)]


def _silu(x):
    h = 0.5 * x
    return h * jnp.tanh(h) + h


def _gla_decays(lr, u_ref, ub_ref, tril_ref, triu_ref):
    L = GLA_CHUNK
    tm = lr.shape[0]
    z = _dot(lr.astype(bf16), u_ref[...]) + ub_ref[...]
    la = (jnp.minimum(z, 0.0) - jnp.log(1.0 + jnp.exp(-jnp.abs(z)))) * (1.0 / GLA_TAU)
    hi = la.astype(bf16)
    lo = (la - hi.astype(f32)).astype(bf16)
    out = []
    for d, (tri_ref, last) in enumerate(((tril_ref, L - 1), (triu_ref, 0))):
        cols = slice(d * GLA_KEY, (d + 1) * GLA_KEY)
        tri = tri_ref[...]
        R = tri.shape[0]
        b = jnp.concatenate(
            [_dot(tri, hi[r:r + R, cols]) + _dot(tri, lo[r:r + R, cols]) for r in range(0, tm, R)],
            axis=0)
        ends = b.reshape(tm // L, L, GLA_KEY)[:, last, :]
        out.append((b, ends))
    return out


def _inproj_kernel(x_ref, wa_ref, wlr_ref, wb_ref, wg_ref, u_ref, ub_ref, tril_ref, triu_ref,
                   gq_ref, v_ref, og_ref, dec_ref, p_ref, qkv_ref, g_ref):
    xb = x_ref[...].astype(bf16)
    ra = _dot(xb, wa_ref[...])
    v_ref[...] = ra[:, 2 * GLA_KEY:2 * GLA_KEY + GLA_VAL].astype(bf16)
    og_ref[...] = ra[:, 2 * GLA_KEY + GLA_VAL:].astype(bf16)
    lr = _dot(xb, wlr_ref[...])
    g_ref[...] = _dot(xb, wg_ref[...]).astype(bf16)
    q = ra[:, :GLA_KEY]
    k = ra[:, GLA_KEY:2 * GLA_KEY]
    for d, (b, ends) in enumerate(_gla_decays(lr, u_ref, ub_ref, tril_ref, triu_ref)):
        o = 2 * d * GLA_KEY
        gq_ref[:, o:o + GLA_KEY] = (q * jnp.exp(b)).astype(bf16)
        gq_ref[:, o + GLA_KEY:o + 2 * GLA_KEY] = (k * jnp.exp(-b)).astype(bf16)
        dec_ref[:, d * GLA_KEY:(d + 1) * GLA_KEY] = jnp.exp(ends)
    rb = _dot(xb, wb_ref[...])
    p_ref[...] = rb[:, :POOL_WIDTH].astype(bf16)
    qkv_ref[...] = rb[:, POOL_WIDTH:].astype(bf16)


def _inproj(x, wa, wlr, wb, wg, u, ub, cst, tm):
    B, T, D = x.shape
    assert tm % CUMSUM_BLOCK == 0
    n_chunks = tm // GLA_CHUNK
    widths = (4 * GLA_KEY, GLA_VAL, GLA_VAL, None, POOL_WIDTH, 3 * NA_WIDTH, N_BRANCH * D_MODEL)
    tok = lambda w: pl.BlockSpec((None, tm, w), lambda b, i: (b, i, 0))
    dec_spec = pl.BlockSpec((None, n_chunks, 2 * GLA_KEY), lambda b, i: (b, i, 0))
    consts = (wa, wlr, wb, wg, u, ub, cst["tril"], cst["triu"])
    return pl.pallas_call(
        _inproj_kernel,
        grid=(B, T // tm),
        in_specs=[tok(D)] + [_const_spec(c.shape) for c in consts],
        out_specs=[dec_spec if w is None else tok(w) for w in widths],
        out_shape=[jax.ShapeDtypeStruct((B, T // GLA_CHUNK, 2 * GLA_KEY), f32) if w is None
                   else jax.ShapeDtypeStruct((B, T, w), bf16) for w in widths],
        compiler_params=_params(("parallel", "parallel")),
        name="inproj",
    )(x, *consts)


def _inproj_constants():
    L, n = GLA_CHUNK, CUMSUM_BLOCK // GLA_CHUNK
    i = np.arange(L)
    tril = (i[None, :] <= i[:, None]).astype(np.float32)
    eye = np.eye(n, dtype=np.float32)
    return dict(tril=jnp.asarray(np.kron(eye, tril), bf16), triu=jnp.asarray(np.kron(eye, tril.T), bf16))


def _gla_inter(qc, kic, vc, dec, st, mkk):
    L, H = GLA_CHUNK, GLA_HEADS
    kec = (kic.astype(f32) * dec).astype(bf16)
    q4 = jnp.concatenate([qc] * H, axis=0) * mkk
    oi = _dot(q4, st.astype(bf16), _NT)
    inter = jnp.concatenate([oi[h * L:(h + 1) * L, (h // 2) * GLA_DV:(h // 2 + 1) * GLA_DV] for h in range(H)],
                            axis=1)
    ke4 = jnp.concatenate([kec] * H, axis=0) * mkk
    zero = jnp.zeros((L, GLA_DV), bf16)
    v4 = jnp.concatenate(
        [jnp.concatenate([vc[:, h * GLA_DV:(h + 1) * GLA_DV], zero] if h // 2 == 0 else
                         [zero, vc[:, h * GLA_DV:(h + 1) * GLA_DV]], axis=1) for h in range(H)],
        axis=0)
    return inter, dec * st + _dot(v4, ke4, _TN)


def _gla_super(sc, gq_ref, v_ref, dec_ref, cml_ref, cmu_ref, mkk_ref, mkv_ref, st_ref, fwd):
    L, NS, H = GLA_CHUNK, GLA_SUPER, GLA_HEADS
    R = L * NS
    r0 = pl.multiple_of(sc * R, R)
    o = 0 if fwd else 2 * GLA_KEY
    q_dec = gq_ref[pl.ds(r0, R), o:o + GLA_KEY]
    k_inv = gq_ref[pl.ds(r0, R), o + GLA_KEY:o + 2 * GLA_KEY]
    v = v_ref[pl.ds(r0, R), :]
    if fwd:
        q_oth = gq_ref[pl.ds(r0, R), 2 * GLA_KEY:3 * GLA_KEY]
        k_oth = gq_ref[pl.ds(r0, R), 3 * GLA_KEY:4 * GLA_KEY]
    dcol = 0 if fwd else GLA_KEY
    mkk = mkk_ref[...]
    st = st_ref[...]
    outs = [None] * NS
    for c in (range(NS) if fwd else range(NS - 1, -1, -1)):
        rows = slice(c * L, (c + 1) * L)
        qc, kic, vc = q_dec[rows], k_inv[rows], v[rows]
        dec = dec_ref[pl.ds(sc * NS + c, 1), dcol:dcol + GLA_KEY]
        o_c, st = _gla_inter(qc, kic, vc, dec, st, mkk)
        if fwd:
            kb = jnp.concatenate([kic] * H, axis=0) * mkk
            att_f = _dot(qc, kb, _NT) * cml_ref[...]
            kb = jnp.concatenate([k_oth[rows]] * H, axis=0) * mkk
            att_b = _dot(q_oth[rows], kb, _NT) * cmu_ref[...]
            vb = jnp.concatenate([vc] * H, axis=0) * mkv_ref[...]
            o2 = _dot(jnp.concatenate([att_f, att_b], axis=0).astype(bf16), vb)
            o_c = o_c + (o2[0:L] + o2[L:2 * L])
        outs[c] = o_c
    st_ref[...] = st
    return jnp.concatenate(outs, axis=0)


def _gla_program(gq_ref, v_ref, dec_ref, ng_ref, cml_ref, cmu_ref, mkk_ref, mkv_ref,
                 o_ref, acc_ref, sf_ref, sb_ref, n_super):
    R = GLA_CHUNK * GLA_SUPER
    shared = (mkk_ref, mkv_ref)

    def init():
        sf_ref[...] = jnp.zeros_like(sf_ref)
        sb_ref[...] = jnp.zeros_like(sb_ref)

    def both(i):
        cf, cb = i, n_super - 1 - i
        of = _gla_super(cf, gq_ref, v_ref, dec_ref, cml_ref, cmu_ref, *shared, sf_ref, True)
        ob = _gla_super(cb, gq_ref, v_ref, dec_ref, cml_ref, cmu_ref, *shared, sb_ref, False)
        return cf, cb, of, ob

    def first_half(i):
        cf, cb, of, ob = both(i)
        acc_ref[pl.ds(pl.multiple_of(cf * R, R), R), :] = of
        acc_ref[pl.ds(pl.multiple_of(cb * R, R), R), :] = ob

    def finish(c, o_new):
        r0 = pl.multiple_of(c * R, R)
        o = acc_ref[pl.ds(r0, R), :] + o_new
        outs = []
        for h in range(GLA_HEADS):
            oh = o[:, h * GLA_DV:(h + 1) * GLA_DV]
            ms = jnp.mean(oh * oh, axis=-1, keepdims=True)
            outs.append(oh * lax.rsqrt(ms + RMS_EPS) * ng_ref[...])
        o_ref[pl.ds(r0, R), :] = jnp.concatenate(outs, axis=1).astype(o_ref.dtype)

    def second_half(i):
        cf, cb, of, ob = both(i)
        finish(cf, of)
        finish(cb, ob)

    return init, first_half, second_half


def _gla_constants():
    L, H = GLA_CHUNK, GLA_HEADS
    i = np.arange(L)
    tril = (i[None, :] <= i[:, None]).astype(np.float32)
    hk = np.arange(H * L) // L
    hv = np.arange(H * GLA_DV) // GLA_DV
    mkk = (hk[:, None] == hk[None, :]).astype(np.float32)
    mkv = (hk[:, None] == hv[None, :]).astype(np.float32)
    return dict(
        cml=jnp.asarray(np.tile(tril, (1, H)), f32), cmu=jnp.asarray(np.tile(tril.T, (1, H)), f32),
        mkk=jnp.asarray(mkk, bf16), mkv=jnp.asarray(mkv, bf16))


def _napool_program(qkv_ref, p_ref, bias_ref, mkk_ref, hm_ref, a_ref, half_ref, pw_ref, ps_ref,
                    yc_ref, yb_ref, s0_ref, s1_ref, e0_ref, e1_ref, d0_ref, d1_ref, rows, seq_len):
    W = GRID_W
    wr = NA_WIN_R

    def window(r):
        r0 = jnp.clip(r - wr // 2, 0, rows - wr)
        return r0, pl.multiple_of(r0 * W, W)

    def scores(r, s_ref):
        r = jnp.minimum(r, rows - 1)
        t0 = pl.multiple_of(r * W, W)
        r0, k0 = window(r)
        q = qkv_ref[pl.ds(t0, W), 0:NA_WIDTH]
        kw = qkv_ref[pl.ds(k0, wr * W), NA_WIDTH:2 * NA_WIDTH]
        qs = jnp.concatenate([q] * NA_HEADS, axis=0) * mkk_ref[...]
        s_ref[...] = _dot(qs, kw, _NT) + bias_ref[r - r0]

    def softmax(s_ref, e_ref, d_ref):
        s = s_ref[...]
        e = jnp.exp2(s - jnp.max(s, axis=-1, keepdims=True))
        d_ref[...] = 1.0 / jnp.sum(e, axis=-1, keepdims=True)
        e_ref[...] = e.astype(bf16)

    def finish(r, e_ref, d_ref):
        t0 = pl.multiple_of(r * W, W)
        _, k0 = window(r)
        vw = qkv_ref[pl.ds(k0, wr * W), 2 * NA_WIDTH:3 * NA_WIDTH]
        pv = _dot(e_ref[...], vw) * d_ref[...]
        hm = hm_ref[...]
        out = pv[0:W] * hm[0:1]
        for h in range(1, NA_HEADS):
            out = out + pv[h * W:(h + 1) * W] * hm[h:h + 1]
        yc_ref[pl.ds(t0, W), :] = out.astype(yc_ref.dtype)

    def pool(r):
        t0 = pl.multiple_of(r * W, W)
        blocks = []
        for off in (-1, 0, 1, 2):
            rr = r + off
            blk = p_ref[pl.ds(pl.multiple_of(jnp.clip(rr, 0, rows - 1) * W, W), W), :]
            if off in (-1, 2):
                blk = jnp.where((rr >= 0) & (rr < rows), blk, jnp.zeros_like(blk))
            blocks.append(blk)
        p4 = jnp.concatenate(blocks, axis=0)
        wall = _dot(a_ref[...], p4)
        hm = hm_ref[...]
        half = half_ref[...]
        ds = []
        for i in range(2):
            base = i * POOL_GROUPS * W
            wsum = wall[base:base + W] * hm[0:1]
            for g in range(1, POOL_GROUPS):
                wsum = wsum + wall[base + g * W:base + (g + 1) * W] * hm[g:g + 1]
            t = t0 + i * W + lax.broadcasted_iota(jnp.int32, (W, POOL_WIDTH), 0)
            cnt = (jnp.minimum(t + half, seq_len) - jnp.maximum(t - half, 0)).astype(f32)
            ds.append((wsum / cnt - blocks[1 + i].astype(f32)).astype(bf16))
        yb = _dot(jnp.concatenate(ds, axis=0), pw_ref[...]) * ps_ref[...]
        yb_ref[pl.ds(t0, 2 * W), :] = yb.astype(yb_ref.dtype)

    def prologue():
        scores(0, s0_ref)
        softmax(s0_ref, e0_ref, d0_ref)
        scores(1, s1_ref)

    def step(j):
        r = 2 * j
        finish(r, e0_ref, d0_ref)
        softmax(s1_ref, e1_ref, d1_ref)
        scores(r + 2, s0_ref)
        finish(r + 1, e1_ref, d1_ref)
        softmax(s0_ref, e0_ref, d0_ref)
        scores(r + 3, s1_ref)
        pool(r)

    return prologue, step


def _napool_constants():
    W = GRID_W
    hk = np.arange(NA_WIDTH) // NA_HD
    hm = (np.arange(NA_HEADS)[:, None] == hk[None, :]).astype(np.float32)
    a = np.zeros((2, POOL_GROUPS, W, 4 * W), np.float32)
    t = np.arange(W)[:, None]
    for i in range(2):
        j = np.arange(4 * W)[None, :] - (1 + i) * W
        for g, w in enumerate(POOL_WINDOWS):
            a[i, g] = ((j >= t - w // 2) & (j < t + w - w // 2)).astype(np.float32)
    half = np.repeat(np.array([w // 2 for w in POOL_WINDOWS], np.int32), POOL_GC)[None, :]
    a = a.reshape(2 * POOL_GROUPS * W, 4 * W)
    return dict(hm=jnp.asarray(hm, f32), a=jnp.asarray(a, bf16), half=jnp.asarray(half, jnp.int32))


def _na_bias_table(rpb):
    W, R, C = GRID_W, NA_WIN_R, NA_WIN_C
    cols = np.arange(W)
    c0 = np.clip(cols - C // 2, 0, W - C)
    valid = (cols[None, :] >= c0[:, None]) & (cols[None, :] < c0[:, None] + C)
    row_sel = np.zeros((R, R, 2 * R - 1), np.float32)
    d, kr = np.meshgrid(np.arange(R), np.arange(R), indexing="ij")
    row_sel[d, kr, kr - d + (R - 1)] = 1.0
    col_sel = np.zeros((W, W, 2 * C - 1), np.float32)
    q, kc = np.nonzero(valid)
    col_sel[q, kc, kc - q + (C - 1)] = 1.0
    tab = jnp.einsum("dkr,hrc,qjc->dhqkj", row_sel, rpb * LOG2E, col_sel,
                     precision=lax.Precision.HIGHEST)
    tab = jnp.where(jnp.asarray(valid)[None, None, :, None, :], tab, NEG_BIG)
    return tab.reshape(R, NA_HEADS * W, R * W).astype(f32)


def _step_loop(lo, hi, step):
    def body(i, carry):
        step(i)
        return carry
    lax.fori_loop(lo, hi, body, 0)


def _gla_kernel(*refs, n_super):
    init, first_half, second_half = _gla_program(*refs, n_super)
    init()
    _step_loop(0, n_super // 2, first_half)
    _step_loop(n_super // 2, n_super, second_half)


def _gla(gq, v, dec, ng, cst):
    B, T, _ = gq.shape
    R = GLA_CHUNK * GLA_SUPER
    assert T % (2 * R) == 0
    seq = lambda a: pl.BlockSpec((None,) + a.shape[1:], lambda b: (b, 0, 0))
    consts = (ng, cst["cml"], cst["cmu"], cst["mkk"], cst["mkv"])
    state = pltpu.VMEM((GLA_HEADS // 2 * GLA_DV, GLA_KEY), f32)
    return pl.pallas_call(
        functools.partial(_gla_kernel, n_super=T // R),
        grid=(B,),
        in_specs=[seq(gq), seq(v), seq(dec)] + [_const_spec(c.shape) for c in consts],
        out_specs=pl.BlockSpec((None, T, GLA_VAL), lambda b: (b, 0, 0)),
        out_shape=jax.ShapeDtypeStruct((B, T, GLA_VAL), bf16),
        scratch_shapes=[pltpu.VMEM((T, GLA_VAL), f32), state, state],
        compiler_params=_params(("parallel",)),
        name="gla",
    )(gq, v, dec, *consts)


def _napool_kernel(*refs, rows, seq_len):
    prologue, step = _napool_program(*refs, rows, seq_len)
    prologue()
    def two_steps(j):
        step(2 * j)
        step(2 * j + 1)

    _step_loop(0, rows // 4, two_steps)


def _napool(qkv, p, bias, mkk, pw, ps, cst):
    B, T, _ = qkv.shape
    rows = T // GRID_W
    assert rows >= NA_WIN_R and rows % 4 == 0
    seq = lambda w: pl.BlockSpec((None, T, w), lambda b: (b, 0, 0))
    consts = (bias, mkk, cst["hm"], cst["a"], cst["half"], pw, ps)
    sc = (NA_HEADS * GRID_W, NA_WIN_R * GRID_W)
    return pl.pallas_call(
        functools.partial(_napool_kernel, rows=rows, seq_len=T),
        grid=(B,),
        in_specs=[seq(3 * NA_WIDTH), seq(POOL_WIDTH)] + [_const_spec(c.shape) for c in consts],
        out_specs=[seq(NA_WIDTH), seq(POOL_WIDTH)],
        out_shape=[jax.ShapeDtypeStruct((B, T, NA_WIDTH), bf16),
                   jax.ShapeDtypeStruct((B, T, POOL_WIDTH), bf16)],
        scratch_shapes=[pltpu.VMEM(sc, f32), pltpu.VMEM(sc, f32), pltpu.VMEM(sc, bf16), pltpu.VMEM(sc, bf16),
                        pltpu.VMEM((sc[0], 1), f32), pltpu.VMEM((sc[0], 1), f32)],
        compiler_params=_params(("parallel",)),
        name="napool",
    )(qkv, p, *consts)


def _merge_kernel(x_ref, on_ref, og_ref, yb_ref, yc_ref, g_ref, wa_ref, wb_ref, wc_ref, wo_ref,
                  lg_ref, lb_ref, o_ref):
    D = D_MODEL
    one = jnp.ones((), bf16)
    for rows in _row_pieces(x_ref.shape[0]):
        h = og_ref[rows, :]
        a_in = on_ref[rows, :] * (h * jnp.tanh(h) + h)
        m = ((one + jnp.tanh(g_ref[rows, 0:D])) * _dot(a_in, wa_ref[...]).astype(bf16)
             + (one + jnp.tanh(g_ref[rows, D:2 * D])) * _dot(yb_ref[rows, :], wb_ref[...]).astype(bf16)
             + (one + jnp.tanh(g_ref[rows, 2 * D:3 * D])) * _dot(yc_ref[rows, :], wc_ref[...]).astype(bf16))
        y = DEEPNORM_ALPHA * x_ref[rows, :] + _dot(m, wo_ref[...])
        o_ref[rows, :] = _layer_norm(y, lg_ref[...], lb_ref[...])


def _merge(x, on, og, yb, yc, g, wa, wb, wc, wo, lg, lb, tm):
    B, T, D = x.shape
    tok = lambda w: pl.BlockSpec((None, tm, w), lambda b, i: (b, i, 0))
    consts = (wa, wb, wc, wo, lg, lb)
    return pl.pallas_call(
        _merge_kernel,
        grid=(B, T // tm),
        in_specs=[tok(D), tok(GLA_VAL), tok(GLA_VAL), tok(POOL_WIDTH), tok(NA_WIDTH),
                  tok(N_BRANCH * D)] + [_const_spec(c.shape) for c in consts],
        out_specs=tok(D),
        out_shape=jax.ShapeDtypeStruct((B, T, D), f32),
        compiler_params=_params(("parallel", "parallel")),
        name="merge",
    )(x, on, og, yb, yc, g, *consts)


def _ffn_kernel(x_ref, wg_ref, wu_ref, wd_ref, lg_ref, lb_ref, o_ref):
    for rows in _row_pieces(x_ref.shape[0]):
        x = x_ref[rows, :]
        xb = x.astype(bf16)
        hg = _dot(xb, wg_ref[...])
        hu = _dot(xb, wu_ref[...])
        act = (_silu(hg) * hu).astype(bf16)
        y = DEEPNORM_ALPHA * x + _dot(act, wd_ref[...])
        o_ref[rows, :] = _layer_norm(y, lg_ref[...], lb_ref[...])


def _ffn(x, wg, wu, wd, lg, lb, tm):
    B, T, D = x.shape
    tok = pl.BlockSpec((None, tm, D), lambda b, i: (b, i, 0))
    consts = (wg, wu, wd, lg, lb)
    return pl.pallas_call(
        _ffn_kernel,
        grid=(B, T // tm),
        in_specs=[tok] + [_const_spec(c.shape) for c in consts],
        out_specs=tok,
        out_shape=jax.ShapeDtypeStruct((B, T, D), f32),
        compiler_params=_params(("parallel", "parallel")),
        name="ffn",
    )(x, *consts)


def _prepare_layer(l, w_in, gla_up_f, gla_up_b, gla_bias_f, gla_bias_b, gla_norm, pool_w, pool_scale,
                   na_rpb, w_br_a, w_br_b, w_br_c, w_out, ln1_g, ln1_b, w_gate, w_up, w_down,
                   ln2_g, ln2_b):
    wi = w_in[l]
    o_lr = 2 * GLA_KEY + 2 * GLA_VAL
    o_p = o_lr + 2 * GLA_RANK
    o_g = o_p + POOL_WIDTH + 3 * NA_WIDTH
    wlr = jnp.zeros((D_MODEL, LR_PAD), f32).at[:, :2 * GLA_RANK].set(wi[:, o_lr:o_p])
    u = jnp.zeros((LR_PAD, 2 * GLA_KEY), f32)
    u = u.at[:GLA_RANK, :GLA_KEY].set(gla_up_f[l]).at[GLA_RANK:2 * GLA_RANK, GLA_KEY:].set(gla_up_b[l])
    row = lambda a: a.reshape(1, -1).astype(f32)
    o_og = 2 * GLA_KEY + GLA_VAL
    o_cq = o_p + POOL_WIDTH
    wb = jnp.concatenate([wi[:, o_p:o_cq], (NA_HD ** -0.5 * LOG2E) * wi[:, o_cq:o_cq + NA_WIDTH],
                          wi[:, o_cq + NA_WIDTH:o_g]], axis=1)
    wa = jnp.concatenate([(GLA_DK ** -0.5) * wi[:, :GLA_KEY], wi[:, GLA_KEY:o_og], 0.5 * wi[:, o_og:o_lr]], axis=1)
    return dict(
        wa=wa.astype(bf16), wlr=wlr.astype(bf16), wb=wb.astype(bf16),
        wg=(0.5 * wi[:, o_g:]).astype(bf16),
        u=u.astype(bf16), ub=row(jnp.concatenate([gla_bias_f[l], gla_bias_b[l]])),
        ng=row(gla_norm[l]),
        pw=jax.scipy.linalg.block_diag(*[pool_w[l, g] for g in range(POOL_GROUPS)]).astype(bf16),
        ps=row(pool_scale[l]), bias=_na_bias_table(na_rpb[l]),
        bra=(0.5 * w_br_a[l]).astype(bf16), brb=(0.5 * w_br_b[l]).astype(bf16),
        brc=(0.5 * w_br_c[l]).astype(bf16),
        wo=w_out[l].astype(bf16), ln1g=row(ln1_g[l]), ln1b=row(ln1_b[l]),
        wgate=w_gate[l].astype(bf16), wup=w_up[l].astype(bf16), wdown=w_down[l].astype(bf16),
        ln2g=row(ln2_g[l]), ln2b=row(ln2_b[l]))


def _trunk(x, layers, in_cst, gla_cst, np_cst):
    for p in layers:
        gq, v, og, dec, pin, qkv, gates = _inproj(x, p["wa"], p["wlr"], p["wb"], p["wg"], p["u"], p["ub"],
                                                  in_cst, INPROJ_TILE)
        on = _gla(gq, v, dec, p["ng"], gla_cst)
        yc, yb = _napool(qkv, pin, p["bias"], gla_cst["mkk"], p["pw"], p["ps"], np_cst)
        x = _merge(x, on, og, yb, yc, gates, p["bra"], p["brb"], p["brc"], p["wo"],
                   p["ln1g"], p["ln1b"], DENSE_TILE)
        x = _ffn(x, p["wgate"], p["wup"], p["wdown"], p["ln2g"], p["ln2b"], DENSE_TILE)
    return x


def kernel(x_prompt, x_sample, w_in, gla_up_f, gla_up_b, gla_bias_f, gla_bias_b, gla_norm, pool_w, pool_scale, na_rpb, w_br_a, w_br_b, w_br_c, w_out, ln1_g, ln1_b, w_gate, w_up, w_down, ln2_g, ln2_b):
    weights = (w_in, gla_up_f, gla_up_b, gla_bias_f, gla_bias_b, gla_norm, pool_w, pool_scale, na_rpb,
               w_br_a, w_br_b, w_br_c, w_out, ln1_g, ln1_b, w_gate, w_up, w_down, ln2_g, ln2_b)
    layers = [_prepare_layer(l, *weights) for l in range(w_in.shape[0])]
    in_cst = _inproj_constants()
    gla_cst = _gla_constants()
    np_cst = _napool_constants()
    return (_trunk(x_prompt, layers, in_cst, gla_cst, np_cst),
            _trunk(x_sample, layers, in_cst, gla_cst, np_cst))
```

```python
import functools

import numpy as np
import jax
import jax.numpy as jnp
from jax import lax
from jax.experimental import pallas as pl
from jax.experimental.pallas import tpu as pltpu

f32 = jnp.float32
bf16 = jnp.bfloat16

D_MODEL = 1024
DEPTH = 4
GRID_W = 64
GLA_HEADS = 4
GLA_DK = 64
GLA_DV = 128
GLA_KEY = GLA_HEADS * GLA_DK
GLA_VAL = GLA_HEADS * GLA_DV
GLA_RANK = 16
GLA_TAU = 16.0
GLA_CHUNK = 64
GLA_SUPER = 8
CUMSUM_BLOCK = 256
POOL_GROUPS = 4
POOL_GC = 64
POOL_WIDTH = POOL_GROUPS * POOL_GC
POOL_WINDOWS = (2, 4, 8, 16)
NA_HEADS = 4
NA_HD = 64
NA_WIDTH = NA_HEADS * NA_HD
NA_WIN_R = 8
NA_WIN_C = 16
N_BRANCH = 3
DEEPNORM_ALPHA = (2 * DEPTH) ** 0.25
LN_EPS = 1e-5
RMS_EPS = 1e-6
LR_PAD = 128
NEG_BIG = -1e30
LOG2E = 1.4426950408889634

DENSE_TILE = 1024
ROW_PIECE = 256
INPROJ_TILE = 512

VMEM_LIMIT_BYTES = 56 * 1024 * 1024

_NN = (((1,), (0,)), ((), ()))
_NT = (((1,), (1,)), ((), ()))
_TN = (((0,), (0,)), ((), ()))


def _dot(a, b, dims=_NN):
    return lax.dot_general(a, b, dims, preferred_element_type=f32)


def _const_spec(shape):
    nd = len(shape)
    return pl.BlockSpec(shape, lambda *_: (0,) * nd, pipeline_mode=pl.Buffered(1))


def _params(sem):
    return pltpu.CompilerParams(dimension_semantics=sem, vmem_limit_bytes=VMEM_LIMIT_BYTES)


def _layer_norm(y, g, b):
    mu = jnp.mean(y, axis=-1, keepdims=True)
    yc = y - mu
    var = jnp.mean(yc * yc, axis=-1, keepdims=True)
    return yc * lax.rsqrt(var + LN_EPS) * g + b


def _row_pieces(tm):
    return [pl.ds(r, ROW_PIECE) for r in range(0, tm, ROW_PIECE)]


def _silu(x):
    h = 0.5 * x
    return h * jnp.tanh(h) + h


def _gla_decays(lr, u_ref, ub_ref, tril_ref, triu_ref):
    L = GLA_CHUNK
    tm = lr.shape[0]
    z = _dot(lr.astype(bf16), u_ref[...]) + ub_ref[...]
    la = (jnp.minimum(z, 0.0) - jnp.log(1.0 + jnp.exp(-jnp.abs(z)))) * (1.0 / GLA_TAU)
    hi = la.astype(bf16)
    lo = (la - hi.astype(f32)).astype(bf16)
    out = []
    for d, (tri_ref, last) in enumerate(((tril_ref, L - 1), (triu_ref, 0))):
        cols = slice(d * GLA_KEY, (d + 1) * GLA_KEY)
        tri = tri_ref[...]
        R = tri.shape[0]
        b = jnp.concatenate(
            [_dot(tri, hi[r:r + R, cols]) + _dot(tri, lo[r:r + R, cols]) for r in range(0, tm, R)],
            axis=0)
        ends = b.reshape(tm // L, L, GLA_KEY)[:, last, :]
        out.append((b, ends))
    return out


def _inproj_kernel(x_ref, wa_ref, wlr_ref, wb_ref, wg_ref, u_ref, ub_ref, tril_ref, triu_ref,
                   gq_ref, v_ref, og_ref, dec_ref, p_ref, qkv_ref, g_ref):
    xb = x_ref[...].astype(bf16)
    ra = _dot(xb, wa_ref[...])
    v_ref[...] = ra[:, 2 * GLA_KEY:2 * GLA_KEY + GLA_VAL].astype(bf16)
    og_ref[...] = ra[:, 2 * GLA_KEY + GLA_VAL:].astype(bf16)
    lr = _dot(xb, wlr_ref[...])
    g_ref[...] = _dot(xb, wg_ref[...]).astype(bf16)
    q = ra[:, :GLA_KEY]
    k = ra[:, GLA_KEY:2 * GLA_KEY]
    for d, (b, ends) in enumerate(_gla_decays(lr, u_ref, ub_ref, tril_ref, triu_ref)):
        o = 2 * d * GLA_KEY
        gq_ref[:, o:o + GLA_KEY] = (q * jnp.exp(b)).astype(bf16)
        gq_ref[:, o + GLA_KEY:o + 2 * GLA_KEY] = (k * jnp.exp(-b)).astype(bf16)
        dec_ref[:, d * GLA_KEY:(d + 1) * GLA_KEY] = jnp.exp(ends)
    rb = _dot(xb, wb_ref[...])
    p_ref[...] = rb[:, :POOL_WIDTH].astype(bf16)
    qkv_ref[...] = rb[:, POOL_WIDTH:].astype(bf16)


def _inproj(x, wa, wlr, wb, wg, u, ub, cst, tm):
    B, T, D = x.shape
    assert tm % CUMSUM_BLOCK == 0
    n_chunks = tm // GLA_CHUNK
    widths = (4 * GLA_KEY, GLA_VAL, GLA_VAL, None, POOL_WIDTH, 3 * NA_WIDTH, N_BRANCH * D_MODEL)
    tok = lambda w: pl.BlockSpec((None, tm, w), lambda b, i: (b, i, 0))
    dec_spec = pl.BlockSpec((None, n_chunks, 2 * GLA_KEY), lambda b, i: (b, i, 0))
    consts = (wa, wlr, wb, wg, u, ub, cst["tril"], cst["triu"])
    return pl.pallas_call(
        _inproj_kernel,
        grid=(B, T // tm),
        in_specs=[tok(D)] + [_const_spec(c.shape) for c in consts],
        out_specs=[dec_spec if w is None else tok(w) for w in widths],
        out_shape=[jax.ShapeDtypeStruct((B, T // GLA_CHUNK, 2 * GLA_KEY), f32) if w is None
                   else jax.ShapeDtypeStruct((B, T, w), bf16) for w in widths],
        compiler_params=_params(("parallel", "parallel")),
        name="inproj",
    )(x, *consts)


def _inproj_constants():
    L, n = GLA_CHUNK, CUMSUM_BLOCK // GLA_CHUNK
    i = np.arange(L)
    tril = (i[None, :] <= i[:, None]).astype(np.float32)
    eye = np.eye(n, dtype=np.float32)
    return dict(tril=jnp.asarray(np.kron(eye, tril), bf16), triu=jnp.asarray(np.kron(eye, tril.T), bf16))


def _gla_inter(qc, kic, vc, dec, st, mkk):
    L, H = GLA_CHUNK, GLA_HEADS
    kec = (kic.astype(f32) * dec).astype(bf16)
    q4 = jnp.concatenate([qc] * H, axis=0) * mkk
    oi = _dot(q4, st.astype(bf16), _NT)
    inter = jnp.concatenate([oi[h * L:(h + 1) * L, (h // 2) * GLA_DV:(h // 2 + 1) * GLA_DV] for h in range(H)],
                            axis=1)
    ke4 = jnp.concatenate([kec] * H, axis=0) * mkk
    zero = jnp.zeros((L, GLA_DV), bf16)
    v4 = jnp.concatenate(
        [jnp.concatenate([vc[:, h * GLA_DV:(h + 1) * GLA_DV], zero] if h // 2 == 0 else
                         [zero, vc[:, h * GLA_DV:(h + 1) * GLA_DV]], axis=1) for h in range(H)],
        axis=0)
    return inter, dec * st + _dot(v4, ke4, _TN)


def _gla_super(sc, gq_ref, v_ref, dec_ref, cml_ref, cmu_ref, mkk_ref, mkv_ref, st_ref, fwd):
    L, NS, H = GLA_CHUNK, GLA_SUPER, GLA_HEADS
    R = L * NS
    r0 = pl.multiple_of(sc * R, R)
    o = 0 if fwd else 2 * GLA_KEY
    q_dec = gq_ref[pl.ds(r0, R), o:o + GLA_KEY]
    k_inv = gq_ref[pl.ds(r0, R), o + GLA_KEY:o + 2 * GLA_KEY]
    v = v_ref[pl.ds(r0, R), :]
    if fwd:
        q_oth = gq_ref[pl.ds(r0, R), 2 * GLA_KEY:3 * GLA_KEY]
        k_oth = gq_ref[pl.ds(r0, R), 3 * GLA_KEY:4 * GLA_KEY]
    dcol = 0 if fwd else GLA_KEY
    mkk = mkk_ref[...]
    st = st_ref[...]
    outs = [None] * NS
    for c in (range(NS) if fwd else range(NS - 1, -1, -1)):
        rows = slice(c * L, (c + 1) * L)
        qc, kic, vc = q_dec[rows], k_inv[rows], v[rows]
        dec = dec_ref[pl.ds(sc * NS + c, 1), dcol:dcol + GLA_KEY]
        o_c, st = _gla_inter(qc, kic, vc, dec, st, mkk)
        if fwd:
            kb = jnp.concatenate([kic] * H, axis=0) * mkk
            att_f = _dot(qc, kb, _NT) * cml_ref[...]
            kb = jnp.concatenate([k_oth[rows]] * H, axis=0) * mkk
            att_b = _dot(q_oth[rows], kb, _NT) * cmu_ref[...]
            vb = jnp.concatenate([vc] * H, axis=0) * mkv_ref[...]
            o2 = _dot(jnp.concatenate([att_f, att_b], axis=0).astype(bf16), vb)
            o_c = o_c + (o2[0:L] + o2[L:2 * L])
        outs[c] = o_c
    st_ref[...] = st
    return jnp.concatenate(outs, axis=0)


def _gla_program(gq_ref, v_ref, dec_ref, ng_ref, cml_ref, cmu_ref, mkk_ref, mkv_ref,
                 o_ref, acc_ref, sf_ref, sb_ref, n_super):
    R = GLA_CHUNK * GLA_SUPER
    shared = (mkk_ref, mkv_ref)

    def init():
        sf_ref[...] = jnp.zeros_like(sf_ref)
        sb_ref[...] = jnp.zeros_like(sb_ref)

    def both(i):
        cf, cb = i, n_super - 1 - i
        of = _gla_super(cf, gq_ref, v_ref, dec_ref, cml_ref, cmu_ref, *shared, sf_ref, True)
        ob = _gla_super(cb, gq_ref, v_ref, dec_ref, cml_ref, cmu_ref, *shared, sb_ref, False)
        return cf, cb, of, ob

    def first_half(i):
        cf, cb, of, ob = both(i)
        acc_ref[pl.ds(pl.multiple_of(cf * R, R), R), :] = of
        acc_ref[pl.ds(pl.multiple_of(cb * R, R), R), :] = ob

    def finish(c, o_new):
        r0 = pl.multiple_of(c * R, R)
        o = acc_ref[pl.ds(r0, R), :] + o_new
        outs = []
        for h in range(GLA_HEADS):
            oh = o[:, h * GLA_DV:(h + 1) * GLA_DV]
            ms = jnp.mean(oh * oh, axis=-1, keepdims=True)
            outs.append(oh * lax.rsqrt(ms + RMS_EPS) * ng_ref[...])
        o_ref[pl.ds(r0, R), :] = jnp.concatenate(outs, axis=1).astype(o_ref.dtype)

    def second_half(i):
        cf, cb, of, ob = both(i)
        finish(cf, of)
        finish(cb, ob)

    return init, first_half, second_half


def _gla_constants():
    L, H = GLA_CHUNK, GLA_HEADS
    i = np.arange(L)
    tril = (i[None, :] <= i[:, None]).astype(np.float32)
    hk = np.arange(H * L) // L
    hv = np.arange(H * GLA_DV) // GLA_DV
    mkk = (hk[:, None] == hk[None, :]).astype(np.float32)
    mkv = (hk[:, None] == hv[None, :]).astype(np.float32)
    return dict(
        cml=jnp.asarray(np.tile(tril, (1, H)), f32), cmu=jnp.asarray(np.tile(tril.T, (1, H)), f32),
        mkk=jnp.asarray(mkk, bf16), mkv=jnp.asarray(mkv, bf16))


def _napool_program(qkv_ref, p_ref, bias_ref, mkk_ref, hm_ref, a_ref, half_ref, pw_ref, ps_ref,
                    yc_ref, yb_ref, s0_ref, s1_ref, e0_ref, e1_ref, d0_ref, d1_ref, rows, seq_len):
    W = GRID_W
    wr = NA_WIN_R

    def window(r):
        r0 = jnp.clip(r - wr // 2, 0, rows - wr)
        return r0, pl.multiple_of(r0 * W, W)

    def scores(r, s_ref):
        r = jnp.minimum(r, rows - 1)
        t0 = pl.multiple_of(r * W, W)
        r0, k0 = window(r)
        q = qkv_ref[pl.ds(t0, W), 0:NA_WIDTH]
        kw = qkv_ref[pl.ds(k0, wr * W), NA_WIDTH:2 * NA_WIDTH]
        qs = jnp.concatenate([q] * NA_HEADS, axis=0) * mkk_ref[...]
        s_ref[...] = _dot(qs, kw, _NT) + bias_ref[r - r0]

    def softmax(s_ref, e_ref, d_ref):
        s = s_ref[...]
        e = jnp.exp2(s - jnp.max(s, axis=-1, keepdims=True))
        d_ref[...] = 1.0 / jnp.sum(e, axis=-1, keepdims=True)
        e_ref[...] = e.astype(bf16)

    def finish(r, e_ref, d_ref):
        t0 = pl.multiple_of(r * W, W)
        _, k0 = window(r)
        vw = qkv_ref[pl.ds(k0, wr * W), 2 * NA_WIDTH:3 * NA_WIDTH]
        pv = _dot(e_ref[...], vw) * d_ref[...]
        hm = hm_ref[...]
        out = pv[0:W] * hm[0:1]
        for h in range(1, NA_HEADS):
            out = out + pv[h * W:(h + 1) * W] * hm[h:h + 1]
        yc_ref[pl.ds(t0, W), :] = out.astype(yc_ref.dtype)

    def pool(r):
        t0 = pl.multiple_of(r * W, W)
        blocks = []
        for off in (-1, 0, 1, 2):
            rr = r + off
            blk = p_ref[pl.ds(pl.multiple_of(jnp.clip(rr, 0, rows - 1) * W, W), W), :]
            if off in (-1, 2):
                blk = jnp.where((rr >= 0) & (rr < rows), blk, jnp.zeros_like(blk))
            blocks.append(blk)
        p4 = jnp.concatenate(blocks, axis=0)
        wall = _dot(a_ref[...], p4)
        hm = hm_ref[...]
        half = half_ref[...]
        ds = []
        for i in range(2):
            base = i * POOL_GROUPS * W
            wsum = wall[base:base + W] * hm[0:1]
            for g in range(1, POOL_GROUPS):
                wsum = wsum + wall[base + g * W:base + (g + 1) * W] * hm[g:g + 1]
            t = t0 + i * W + lax.broadcasted_iota(jnp.int32, (W, POOL_WIDTH), 0)
            cnt = (jnp.minimum(t + half, seq_len) - jnp.maximum(t - half, 0)).astype(f32)
            ds.append((wsum / cnt - blocks[1 + i].astype(f32)).astype(bf16))
        yb = _dot(jnp.concatenate(ds, axis=0), pw_ref[...]) * ps_ref[...]
        yb_ref[pl.ds(t0, 2 * W), :] = yb.astype(yb_ref.dtype)

    def prologue():
        scores(0, s0_ref)
        softmax(s0_ref, e0_ref, d0_ref)
        scores(1, s1_ref)

    def step(j):
        r = 2 * j
        finish(r, e0_ref, d0_ref)
        softmax(s1_ref, e1_ref, d1_ref)
        scores(r + 2, s0_ref)
        finish(r + 1, e1_ref, d1_ref)
        softmax(s0_ref, e0_ref, d0_ref)
        scores(r + 3, s1_ref)
        pool(r)

    return prologue, step


def _napool_constants():
    W = GRID_W
    hk = np.arange(NA_WIDTH) // NA_HD
    hm = (np.arange(NA_HEADS)[:, None] == hk[None, :]).astype(np.float32)
    a = np.zeros((2, POOL_GROUPS, W, 4 * W), np.float32)
    t = np.arange(W)[:, None]
    for i in range(2):
        j = np.arange(4 * W)[None, :] - (1 + i) * W
        for g, w in enumerate(POOL_WINDOWS):
            a[i, g] = ((j >= t - w // 2) & (j < t + w - w // 2)).astype(np.float32)
    half = np.repeat(np.array([w // 2 for w in POOL_WINDOWS], np.int32), POOL_GC)[None, :]
    a = a.reshape(2 * POOL_GROUPS * W, 4 * W)
    return dict(hm=jnp.asarray(hm, f32), a=jnp.asarray(a, bf16), half=jnp.asarray(half, jnp.int32))


def _na_bias_table(rpb):
    W, R, C = GRID_W, NA_WIN_R, NA_WIN_C
    cols = np.arange(W)
    c0 = np.clip(cols - C // 2, 0, W - C)
    valid = (cols[None, :] >= c0[:, None]) & (cols[None, :] < c0[:, None] + C)
    row_sel = np.zeros((R, R, 2 * R - 1), np.float32)
    d, kr = np.meshgrid(np.arange(R), np.arange(R), indexing="ij")
    row_sel[d, kr, kr - d + (R - 1)] = 1.0
    col_sel = np.zeros((W, W, 2 * C - 1), np.float32)
    q, kc = np.nonzero(valid)
    col_sel[q, kc, kc - q + (C - 1)] = 1.0
    tab = jnp.einsum("dkr,hrc,qjc->dhqkj", row_sel, rpb * LOG2E, col_sel,
                     precision=lax.Precision.HIGHEST)
    tab = jnp.where(jnp.asarray(valid)[None, None, :, None, :], tab, NEG_BIG)
    return tab.reshape(R, NA_HEADS * W, R * W).astype(f32)


def _step_loop(lo, hi, step):
    def body(i, carry):
        step(i)
        return carry
    lax.fori_loop(lo, hi, body, 0)


def _gla_kernel(*refs, n_super):
    init, first_half, second_half = _gla_program(*refs, n_super)
    init()
    _step_loop(0, n_super // 2, first_half)
    _step_loop(n_super // 2, n_super, second_half)


def _gla(gq, v, dec, ng, cst):
    B, T, _ = gq.shape
    R = GLA_CHUNK * GLA_SUPER
    assert T % (2 * R) == 0
    seq = lambda a: pl.BlockSpec((None,) + a.shape[1:], lambda b: (b, 0, 0))
    consts = (ng, cst["cml"], cst["cmu"], cst["mkk"], cst["mkv"])
    state = pltpu.VMEM((GLA_HEADS // 2 * GLA_DV, GLA_KEY), f32)
    return pl.pallas_call(
        functools.partial(_gla_kernel, n_super=T // R),
        grid=(B,),
        in_specs=[seq(gq), seq(v), seq(dec)] + [_const_spec(c.shape) for c in consts],
        out_specs=pl.BlockSpec((None, T, GLA_VAL), lambda b: (b, 0, 0)),
        out_shape=jax.ShapeDtypeStruct((B, T, GLA_VAL), bf16),
        scratch_shapes=[pltpu.VMEM((T, GLA_VAL), f32), state, state],
        compiler_params=_params(("parallel",)),
        name="gla",
    )(gq, v, dec, *consts)


def _napool_kernel(*refs, rows, seq_len):
    prologue, step = _napool_program(*refs, rows, seq_len)
    prologue()
    def four_steps(j):
        for t in range(4):
            step(4 * j + t)

    _step_loop(0, rows // 8, four_steps)


def _napool(qkv, p, bias, mkk, pw, ps, cst):
    B, T, _ = qkv.shape
    rows = T // GRID_W
    assert rows >= NA_WIN_R and rows % 8 == 0
    seq = lambda w: pl.BlockSpec((None, T, w), lambda b: (b, 0, 0))
    consts = (bias, mkk, cst["hm"], cst["a"], cst["half"], pw, ps)
    sc = (NA_HEADS * GRID_W, NA_WIN_R * GRID_W)
    return pl.pallas_call(
        functools.partial(_napool_kernel, rows=rows, seq_len=T),
        grid=(B,),
        in_specs=[seq(3 * NA_WIDTH), seq(POOL_WIDTH)] + [_const_spec(c.shape) for c in consts],
        out_specs=[seq(NA_WIDTH), seq(POOL_WIDTH)],
        out_shape=[jax.ShapeDtypeStruct((B, T, NA_WIDTH), bf16),
                   jax.ShapeDtypeStruct((B, T, POOL_WIDTH), bf16)],
        scratch_shapes=[pltpu.VMEM(sc, f32), pltpu.VMEM(sc, f32), pltpu.VMEM(sc, bf16), pltpu.VMEM(sc, bf16),
                        pltpu.VMEM((sc[0], 1), f32), pltpu.VMEM((sc[0], 1), f32)],
        compiler_params=_params(("parallel",)),
        name="napool",
    )(qkv, p, *consts)


def _merge_kernel(x_ref, on_ref, og_ref, yb_ref, yc_ref, g_ref, wa_ref, wb_ref, wc_ref, wo_ref,
                  lg_ref, lb_ref, o_ref):
    D = D_MODEL
    one = jnp.ones((), bf16)
    for rows in _row_pieces(x_ref.shape[0]):
        h = og_ref[rows, :]
        a_in = on_ref[rows, :] * (h * jnp.tanh(h) + h)
        m = ((one + jnp.tanh(g_ref[rows, 0:D])) * _dot(a_in, wa_ref[...]).astype(bf16)
             + (one + jnp.tanh(g_ref[rows, D:2 * D])) * _dot(yb_ref[rows, :], wb_ref[...]).astype(bf16)
             + (one + jnp.tanh(g_ref[rows, 2 * D:3 * D])) * _dot(yc_ref[rows, :], wc_ref[...]).astype(bf16))
        y = DEEPNORM_ALPHA * x_ref[rows, :] + _dot(m, wo_ref[...])
        o_ref[rows, :] = _layer_norm(y, lg_ref[...], lb_ref[...])


def _merge(x, on, og, yb, yc, g, wa, wb, wc, wo, lg, lb, tm):
    B, T, D = x.shape
    tok = lambda w: pl.BlockSpec((None, tm, w), lambda b, i: (b, i, 0))
    consts = (wa, wb, wc, wo, lg, lb)
    return pl.pallas_call(
        _merge_kernel,
        grid=(B, T // tm),
        in_specs=[tok(D), tok(GLA_VAL), tok(GLA_VAL), tok(POOL_WIDTH), tok(NA_WIDTH),
                  tok(N_BRANCH * D)] + [_const_spec(c.shape) for c in consts],
        out_specs=tok(D),
        out_shape=jax.ShapeDtypeStruct((B, T, D), f32),
        compiler_params=_params(("parallel", "parallel")),
        name="merge",
    )(x, on, og, yb, yc, g, *consts)


def _ffn_kernel(x_ref, wg_ref, wu_ref, wd_ref, lg_ref, lb_ref, o_ref):
    for rows in _row_pieces(x_ref.shape[0]):
        x = x_ref[rows, :]
        xb = x.astype(bf16)
        hg = _dot(xb, wg_ref[...])
        hu = _dot(xb, wu_ref[...])
        act = (_silu(hg) * hu).astype(bf16)
        y = DEEPNORM_ALPHA * x + _dot(act, wd_ref[...])
        o_ref[rows, :] = _layer_norm(y, lg_ref[...], lb_ref[...])


def _ffn(x, wg, wu, wd, lg, lb, tm):
    B, T, D = x.shape
    tok = pl.BlockSpec((None, tm, D), lambda b, i: (b, i, 0))
    consts = (wg, wu, wd, lg, lb)
    return pl.pallas_call(
        _ffn_kernel,
        grid=(B, T // tm),
        in_specs=[tok] + [_const_spec(c.shape) for c in consts],
        out_specs=tok,
        out_shape=jax.ShapeDtypeStruct((B, T, D), f32),
        compiler_params=_params(("parallel", "parallel")),
        name="ffn",
    )(x, *consts)


def _prepare_layer(l, w_in, gla_up_f, gla_up_b, gla_bias_f, gla_bias_b, gla_norm, pool_w, pool_scale,
                   na_rpb, w_br_a, w_br_b, w_br_c, w_out, ln1_g, ln1_b, w_gate, w_up, w_down,
                   ln2_g, ln2_b):
    wi = w_in[l]
    o_lr = 2 * GLA_KEY + 2 * GLA_VAL
    o_p = o_lr + 2 * GLA_RANK
    o_g = o_p + POOL_WIDTH + 3 * NA_WIDTH
    wlr = jnp.zeros((D_MODEL, LR_PAD), f32).at[:, :2 * GLA_RANK].set(wi[:, o_lr:o_p])
    u = jnp.zeros((LR_PAD, 2 * GLA_KEY), f32)
    u = u.at[:GLA_RANK, :GLA_KEY].set(gla_up_f[l]).at[GLA_RANK:2 * GLA_RANK, GLA_KEY:].set(gla_up_b[l])
    row = lambda a: a.reshape(1, -1).astype(f32)
    o_og = 2 * GLA_KEY + GLA_VAL
    o_cq = o_p + POOL_WIDTH
    wb = jnp.concatenate([wi[:, o_p:o_cq], (NA_HD ** -0.5 * LOG2E) * wi[:, o_cq:o_cq + NA_WIDTH],
                          wi[:, o_cq + NA_WIDTH:o_g]], axis=1)
    wa = jnp.concatenate([(GLA_DK ** -0.5) * wi[:, :GLA_KEY], wi[:, GLA_KEY:o_og], 0.5 * wi[:, o_og:o_lr]], axis=1)
    return dict(
        wa=wa.astype(bf16), wlr=wlr.astype(bf16), wb=wb.astype(bf16),
        wg=(0.5 * wi[:, o_g:]).astype(bf16),
        u=u.astype(bf16), ub=row(jnp.concatenate([gla_bias_f[l], gla_bias_b[l]])),
        ng=row(gla_norm[l]),
        pw=jax.scipy.linalg.block_diag(*[pool_w[l, g] for g in range(POOL_GROUPS)]).astype(bf16),
        ps=row(pool_scale[l]), bias=_na_bias_table(na_rpb[l]),
        bra=(0.5 * w_br_a[l]).astype(bf16), brb=(0.5 * w_br_b[l]).astype(bf16),
        brc=(0.5 * w_br_c[l]).astype(bf16),
        wo=w_out[l].astype(bf16), ln1g=row(ln1_g[l]), ln1b=row(ln1_b[l]),
        wgate=w_gate[l].astype(bf16), wup=w_up[l].astype(bf16), wdown=w_down[l].astype(bf16),
        ln2g=row(ln2_g[l]), ln2b=row(ln2_b[l]))


def _trunk(x, layers, in_cst, gla_cst, np_cst):
    for p in layers:
        gq, v, og, dec, pin, qkv, gates = _inproj(x, p["wa"], p["wlr"], p["wb"], p["wg"], p["u"], p["ub"],
                                                  in_cst, INPROJ_TILE)
        on = _gla(gq, v, dec, p["ng"], gla_cst)
        yc, yb = _napool(qkv, pin, p["bias"], gla_cst["mkk"], p["pw"], p["ps"], np_cst)
        x = _merge(x, on, og, yb, yc, gates, p["bra"], p["brb"], p["brc"], p["wo"],
                   p["ln1g"], p["ln1b"], DENSE_TILE)
        x = _ffn(x, p["wgate"], p["wup"], p["wdown"], p["ln2g"], p["ln2b"], DENSE_TILE)
    return x


def kernel(x_prompt, x_sample, w_in, gla_up_f, gla_up_b, gla_bias_f, gla_bias_b, gla_norm, pool_w, pool_scale, na_rpb, w_br_a, w_br_b, w_br_c, w_out, ln1_g, ln1_b, w_gate, w_up, w_down, ln2_g, ln2_b):
    weights = (w_in, gla_up_f, gla_up_b, gla_bias_f, gla_bias_b, gla_norm, pool_w, pool_scale, na_rpb,
               w_br_a, w_br_b, w_br_c, w_out, ln1_g, ln1_b, w_gate, w_up, w_down, ln2_g, ln2_b)
    layers = [_prepare_layer(l, *weights) for l in range(w_in.shape[0])]
    in_cst = _inproj_constants()
    gla_cst = _gla_constants()
    np_cst = _napool_constants()
    return (_trunk(x_prompt, layers, in_cst, gla_cst, np_cst),
            _trunk(x_sample, layers, in_cst, gla_cst, np_cst))
```
